```python
import math
import jax, jax.numpy as jnp
from jax import lax
import numpy as np


D_MODEL = 1024
BATCH = 4
SEQ = 4096
DEPTH = 1

HEAD_DIM = 64
ATTN_WIDTH = D_MODEL // 2
N_ATTN_HEADS = ATTN_WIDTH // HEAD_DIM
REC_WIDTH = D_MODEL - ATTN_WIDTH
REC_BLOCKS = 8
REC_BLOCK = REC_WIDTH // REC_BLOCKS
MIX_WIDTH = ATTN_WIDTH + REC_WIDTH
IN_WIDTH = 3 * ATTN_WIDTH + 2 * REC_WIDTH
REC_CONV = 4
LRU_C = 8.0
D_FF = 3 * D_MODEL
FFN_CONV = 3
WINDOW_DILATIONS = ((128, 1), (512, 4), (2048, 16))
BLOCK = 128
ROPE_THETA = 10000.0
EPS = 1e-6
NEG_INF = -1e30

kernel_name = "hybrid_dilated_attn_rglru_convffn"


def rms_norm(x, g):
    xf = x.astype(jnp.float32)
    y = xf * lax.rsqrt(jnp.mean(xf * xf, axis=-1, keepdims=True) + EPS)
    return (y * g.astype(jnp.float32)).astype(x.dtype)


def rotary(x, positions):
    half = HEAD_DIM // 2
    inv_freq = ROPE_THETA ** (-jnp.arange(half, dtype=jnp.float32) / half)
    ang = positions.astype(jnp.float32)[..., None] * inv_freq
    cos = jnp.cos(ang)[:, :, None, :]
    sin = jnp.sin(ang)[:, :, None, :]
    xf = x.astype(jnp.float32)
    x1, x2 = xf[..., :half], xf[..., half:]
    return jnp.concatenate([x1 * cos - x2 * sin, x2 * cos + x1 * sin], axis=-1).astype(x.dtype)


def causal_depthwise_conv(x, w, b):
    k_width = w.shape[0]
    s = x.shape[1]
    xp = jnp.pad(x, ((0, 0), (k_width - 1, 0), (0, 0)))
    y = b
    for k in range(k_width):
        y = y + xp[:, k:k + s, :] * w[k]
    return y


def dilated_window_branch(q, k, v, window, dilation):
    bsz, s, h, d = q.shape
    length = s // dilation
    span = window // dilation
    assert span <= BLOCK
    nb = -(-length // BLOCK)
    lp = nb * BLOCK

    def regroup(t):
        return t.reshape(bsz, length, dilation, h, d).transpose(0, 2, 3, 1, 4)

    qs = jnp.pad(regroup(q), ((0, 0), (0, 0), (0, 0), (0, lp - length), (0, 0)))
    ks = jnp.pad(regroup(k), ((0, 0), (0, 0), (0, 0), (BLOCK, lp - length), (0, 0)))
    vs = jnp.pad(regroup(v), ((0, 0), (0, 0), (0, 0), (BLOCK, lp - length), (0, 0)))
    qb = qs.reshape(bsz, dilation, h, nb, BLOCK, d)
    kb = ks.reshape(bsz, dilation, h, nb + 1, BLOCK, d)
    vb = vs.reshape(bsz, dilation, h, nb + 1, BLOCK, d)
    kwin = jnp.concatenate([kb[:, :, :, :-1], kb[:, :, :, 1:]], axis=4)
    vwin = jnp.concatenate([vb[:, :, :, :-1], vb[:, :, :, 1:]], axis=4)

    scores = jnp.einsum('bchnqd,bchnkd->bchnqk', qb, kwin).astype(jnp.float32)
    qi = jnp.arange(BLOCK)[:, None]
    kj = jnp.arange(2 * BLOCK)[None, :]
    rel = qi - kj + BLOCK
    band = (rel >= 0) & (rel <= span)
    blk = jnp.arange(nb)[:, None, None]
    key_ok = (blk * BLOCK + kj[None] - BLOCK) >= 0
    mask = band[None] & key_ok
    scores = jnp.where(mask, scores, NEG_INF)
    m = jnp.max(scores, axis=-1, keepdims=True)
    p = jnp.exp(scores - m)
    l = jnp.sum(p, axis=-1, keepdims=True)
    o = jnp.einsum('bchnqk,bchnkd->bchnqd', p, vwin.astype(jnp.float32)) / l
    lse = (m + jnp.log(l))[..., 0]

    o = o.reshape(bsz, dilation, h, lp, d)[:, :, :, :length]
    lse = lse.reshape(bsz, dilation, h, lp)[:, :, :, :length]
    o = o.transpose(0, 3, 1, 2, 4).reshape(bsz, s, h, d)
    lse = lse.transpose(0, 3, 1, 2).reshape(bsz, s, h)
    return o, lse


def dilated_attention(q, k, v):
    outs, lses = [], []
    for window, dilation in WINDOW_DILATIONS:
        o, lse = dilated_window_branch(q, k, v, window, dilation)
        outs.append(o)
        lses.append(lse)
    wts = jax.nn.softmax(jnp.stack(lses, axis=0), axis=0)
    return jnp.einsum('gbsh,gbshd->bshd', wts, jnp.stack(outs, axis=0))


def lru_combine(left, right):
    a_l, b_l = left
    a_r, b_r = right
    return a_l * a_r, a_r * b_l + b_r


def rg_lru(xr, w_rg, b_rg, w_ig, b_ig, lru_lambda):
    bsz, s, _ = xr.shape
    xb = xr.reshape(bsz, s, REC_BLOCKS, REC_BLOCK)
    r = jax.nn.sigmoid(jnp.einsum('bsnc,ncd->bsnd', xb, w_rg) + b_rg).reshape(bsz, s, REC_WIDTH)
    i = jax.nn.sigmoid(jnp.einsum('bsnc,ncd->bsnd', xb, w_ig) + b_ig).reshape(bsz, s, REC_WIDTH)
    r = r.astype(jnp.float32)
    i = i.astype(jnp.float32)
    log_a = -LRU_C * r * jax.nn.softplus(-lru_lambda.astype(jnp.float32))
    a = jnp.exp(log_a)
    mult = jnp.sqrt(-jnp.expm1(2.0 * log_a))
    u = mult * (i * xr.astype(jnp.float32))
    _, hseq = lax.associative_scan(lru_combine, (a, u), axis=1)
    return hseq.astype(xr.dtype)


def setup_inputs(seed: int = 0) -> dict:
    key = jax.random.key(seed)
    ks = jax.random.split(key, 24)
    f32 = jnp.float32

    def nrm(k, shape, scale):
        return jax.random.normal(k, shape, f32) * scale

    def gain(k, shape):
        return 1.0 + 0.01 * jax.random.normal(k, shape, f32)

    x = jax.random.normal(ks[0], (BATCH, SEQ, D_MODEL), f32)
    positions = jnp.broadcast_to(jnp.arange(SEQ, dtype=jnp.int32)[None, :], (BATCH, SEQ))
    a_c = jax.random.uniform(ks[11], (DEPTH, REC_WIDTH), f32, 0.9, 0.999)
    sig = a_c ** (1.0 / LRU_C)
    lru_lambda = jnp.log(sig) - jnp.log1p(-sig)
    return {
        "x": x,
        "positions": positions,
        "g_mix": gain(ks[1], (DEPTH, D_MODEL)),
        "w_in": nrm(ks[2], (DEPTH, D_MODEL, IN_WIDTH), D_MODEL ** -0.5),
        "q_norm_g": gain(ks[3], (DEPTH, HEAD_DIM)),
        "k_norm_g": gain(ks[4], (DEPTH, HEAD_DIM)),
        "rec_conv_w": nrm(ks[5], (DEPTH, REC_CONV, REC_WIDTH), REC_CONV ** -0.5),
        "rec_conv_b": nrm(ks[6], (DEPTH, REC_WIDTH), 0.01),
        "w_rg": nrm(ks[7], (DEPTH, REC_BLOCKS, REC_BLOCK, REC_BLOCK), REC_BLOCK ** -0.5),
        "b_rg": nrm(ks[8], (DEPTH, REC_BLOCKS, REC_BLOCK), 0.01),
        "w_ig": nrm(ks[9], (DEPTH, REC_BLOCKS, REC_BLOCK, REC_BLOCK), REC_BLOCK ** -0.5),
        "b_ig": nrm(ks[10], (DEPTH, REC_BLOCKS, REC_BLOCK), 0.01),
        "lru_lambda": lru_lambda,
        "g_attn_out": gain(ks[12], (DEPTH, ATTN_WIDTH)),
        "g_rec_out": gain(ks[13], (DEPTH, REC_WIDTH)),
        "w_out": nrm(ks[14], (DEPTH, MIX_WIDTH, D_MODEL), MIX_WIDTH ** -0.5),
        "g_ffn": gain(ks[15], (DEPTH, D_MODEL)),
        "w_up": nrm(ks[16], (DEPTH, D_MODEL, 2 * D_FF), D_MODEL ** -0.5),
        "ffn_conv_w": nrm(ks[17], (DEPTH, FFN_CONV, 2 * D_FF), FFN_CONV ** -0.5),
        "ffn_conv_b": nrm(ks[18], (DEPTH, 2 * D_FF), 0.01),
        "w_down": nrm(ks[19], (DEPTH, D_FF, D_MODEL), D_FF ** -0.5),
    }


def reference(x, positions, g_mix, w_in, q_norm_g, k_norm_g, rec_conv_w, rec_conv_b,
              w_rg, b_rg, w_ig, b_ig, lru_lambda, g_attn_out, g_rec_out, w_out,
              g_ffn, w_up, ffn_conv_w, ffn_conv_b, w_down):
    bsz, s, _ = x.shape
    for layer in range(DEPTH):
        h = rms_norm(x, g_mix[layer])
        proj = h @ w_in[layer]
        q, k, v, xr, gr = jnp.split(
            proj, [ATTN_WIDTH, 2 * ATTN_WIDTH, 3 * ATTN_WIDTH, 3 * ATTN_WIDTH + REC_WIDTH], axis=-1)
        q = q.reshape(bsz, s, N_ATTN_HEADS, HEAD_DIM)
        k = k.reshape(bsz, s, N_ATTN_HEADS, HEAD_DIM)
        v = v.reshape(bsz, s, N_ATTN_HEADS, HEAD_DIM)
        q = rotary(rms_norm(q, q_norm_g[layer]), positions) * (HEAD_DIM ** -0.5)
        k = rotary(rms_norm(k, k_norm_g[layer]), positions)
        attn = dilated_attention(q, k, v).astype(x.dtype).reshape(bsz, s, ATTN_WIDTH)
        attn = rms_norm(attn, g_attn_out[layer])

        xr = causal_depthwise_conv(xr, rec_conv_w[layer], rec_conv_b[layer])
        rec = rg_lru(xr, w_rg[layer], b_rg[layer], w_ig[layer], b_ig[layer], lru_lambda[layer])
        rec = rms_norm(rec * jax.nn.gelu(gr), g_rec_out[layer])

        x = x + jnp.concatenate([attn, rec], axis=-1) @ w_out[layer]

        h = rms_norm(x, g_ffn[layer])
        u = causal_depthwise_conv(h @ w_up[layer], ffn_conv_w[layer], ffn_conv_b[layer])
        gate, up = jnp.split(u, 2, axis=-1)
        x = x + (jax.nn.gelu(gate) * up) @ w_down[layer]
    return x
```

```python
import functools

import jax
import jax.numpy as jnp
import numpy as np
from jax import lax
from jax.experimental import pallas as pl
from jax.experimental.pallas import tpu as pltpu

HEAD_DIM = 64
REC_BLOCKS = 8
REC_CONV = 4
LRU_C = 8.0
FFN_CONV = 3
WINDOW_DILATIONS = ((128, 1), (512, 4), (2048, 16))
BLOCK = 128
ROPE_THETA = 10000.0
EPS = 1e-6
NEG_INF = -1e30

LANES = 128
SUBLANES = 8
VMEM_LIMIT = 56 * 1024 * 1024

F32 = jnp.float32
BF16 = jnp.bfloat16


def _cparams(sem):
    return pltpu.CompilerParams(dimension_semantics=sem, vmem_limit_bytes=VMEM_LIMIT)


def _const_spec(shape):
    nd = len(shape)
    return pl.BlockSpec(shape, lambda *_: (0,) * nd)


def _inproj_kernel(x_ref, pos_ref, gmix_ref, w_ref, bd_ref, qg_ref, kg_ref, invf_ref,
                   q_ref, k_ref, v_ref, xr_ref, gr_ref, *, aw, rw):
    x = x_ref[...]
    ms = jnp.mean(x * x, axis=-1, keepdims=True)
    h = (x * lax.rsqrt(ms + EPS) * gmix_ref[...]).astype(BF16)

    tm = x.shape[0]
    ang = pos_ref[...] * invf_ref[...]
    cos = jnp.cos(ang)
    sin = jnp.sin(ang)
    lane = lax.broadcasted_iota(jnp.int32, (tm, LANES), 1)
    first_half = (lane & (HEAD_DIM // 2)) == 0
    sin_signed = jnp.where(first_half, -sin, sin)

    def head_norm_rotary(p, g_ref):
        ms_h = jnp.dot((p * p).astype(BF16), bd_ref[...], preferred_element_type=F32)
        pn = p * lax.rsqrt(ms_h + EPS) * g_ref[...]
        outs = []
        for g in range(aw // LANES):
            xg = pn[:, g * LANES:(g + 1) * LANES]
            swapped = jnp.where(first_half,
                                pltpu.roll(xg, LANES - HEAD_DIM // 2, 1),
                                pltpu.roll(xg, HEAD_DIM // 2, 1))
            outs.append(xg * cos + swapped * sin_signed)
        return jnp.concatenate(outs, axis=1)

    qp = jnp.dot(h, w_ref[:, 0:aw], preferred_element_type=F32)
    q_ref[...] = (head_norm_rotary(qp, qg_ref) * (HEAD_DIM ** -0.5)).astype(BF16)
    kp = jnp.dot(h, w_ref[:, aw:2 * aw], preferred_element_type=F32)
    k_ref[...] = head_norm_rotary(kp, kg_ref).astype(BF16)
    v_ref[...] = jnp.dot(h, w_ref[:, 2 * aw:3 * aw], preferred_element_type=F32).astype(BF16)
    xr_ref[...] = jnp.dot(h, w_ref[:, 3 * aw:3 * aw + rw], preferred_element_type=F32)
    gr_ref[...] = jnp.dot(h, w_ref[:, 3 * aw + rw:3 * aw + 2 * rw], preferred_element_type=F32)


def _inproj(x2, pos, g_mix, w_in_b, bd, qg, kg, invf, *, aw, rw, tm=512):
    t, d = x2.shape
    row = lambda w: pl.BlockSpec((tm, w), lambda i: (i, 0))
    return pl.pallas_call(
        functools.partial(_inproj_kernel, aw=aw, rw=rw),
        grid=(t // tm,),
        in_specs=[row(d), row(1), _const_spec((1, d)), _const_spec(w_in_b.shape),
                  _const_spec(bd.shape), _const_spec((1, aw)), _const_spec((1, aw)),
                  _const_spec((1, LANES))],
        out_specs=[row(aw), row(aw), row(aw), row(rw), row(rw)],
        out_shape=[jax.ShapeDtypeStruct((t, aw), BF16)] * 3
                  + [jax.ShapeDtypeStruct((t, rw), F32)] * 2,
        compiler_params=_cparams(("parallel",)),
        name="inproj",
    )(x2, pos, g_mix, w_in_b, bd, qg, kg, invf)


def _attn_kernel(q_ref, kh_ref, kc_ref, vh_ref, vc_ref, o_ref, lse_ref, kbuf, vbuf, *, qb, span):
    n = pl.program_id(2)
    kbuf[0:BLOCK, :] = kh_ref[...]
    kbuf[BLOCK:, :] = kc_ref[...]
    vbuf[0:BLOCK, :] = vh_ref[...]
    vbuf[BLOCK:, :] = vc_ref[...]

    qi = lax.broadcasted_iota(jnp.int32, (BLOCK, 2 * BLOCK), 0)
    kj = lax.broadcasted_iota(jnp.int32, (BLOCK, 2 * BLOCK), 1)
    rel = qi - kj + BLOCK
    band = (rel >= 0) & (rel <= span)
    lane = lax.broadcasted_iota(jnp.int32, (BLOCK, LANES), 1)
    low_head = lane < HEAD_DIM

    def body(i, carry):
        r0 = pl.multiple_of(i * BLOCK, BLOCK)
        first_valid = jnp.where((n == 0) & (i == 0), BLOCK, 0)
        mask = band & (kj >= first_valid)
        for g in range(q_ref.shape[1] // LANES):
            cols = slice(g * LANES, (g + 1) * LANES)
            qp = q_ref[pl.ds(r0, BLOCK), cols]
            ks = kbuf[pl.ds(r0, 2 * BLOCK), cols]
            vs = vbuf[pl.ds(r0, 2 * BLOCK), cols]
            outs, lses = [], []
            for sel in (low_head, jnp.logical_not(low_head)):
                qm = jnp.where(sel, qp, jnp.zeros_like(qp))
                s = lax.dot_general(qm, ks, (((1,), (1,)), ((), ())), preferred_element_type=F32)
                s = jnp.where(mask, s, NEG_INF)
                m = jnp.max(s, axis=-1, keepdims=True)
                p = jnp.exp(s - m)
                l = jnp.sum(p, axis=-1, keepdims=True)
                pv = jnp.dot(p.astype(BF16), vs, preferred_element_type=F32)
                outs.append(pv / l)
                lses.append(jnp.broadcast_to(m + jnp.log(l), (BLOCK, LANES)))
            o_ref[pl.ds(r0, BLOCK), cols] = jnp.where(low_head, outs[0], outs[1])
            lse_ref[pl.ds(r0, BLOCK), cols] = jnp.where(low_head, lses[0], lses[1])
        return carry

    lax.fori_loop(0, qb, body, 0)


def _attn_branch(q, k, v, window, dilation):
    b, s, w = q.shape
    length = s // dilation
    span = window // dilation
    assert span <= BLOCK and length % BLOCK == 0
    qb = min(4, length // BLOCK)
    rows = qb * BLOCK
    nsteps = length // rows
    view = lambda t: t.reshape(b, length, dilation * w)
    cur = pl.BlockSpec((None, rows, w), lambda bi, c, n: (bi, n, c))
    halo = pl.BlockSpec((None, BLOCK, w), lambda bi, c, n: (bi, jnp.maximum(n * qb - 1, 0), c))
    o, lse = pl.pallas_call(
        functools.partial(_attn_kernel, qb=qb, span=span),
        grid=(b, dilation, nsteps),
        in_specs=[cur, halo, cur, halo, cur],
        out_specs=[cur, cur],
        out_shape=[jax.ShapeDtypeStruct((b, length, dilation * w), F32)] * 2,
        scratch_shapes=[pltpu.VMEM((rows + BLOCK, w), BF16), pltpu.VMEM((rows + BLOCK, w), BF16)],
        compiler_params=_cparams(("parallel", "parallel", "arbitrary")),
        name=f"attn_d{dilation}",
    )(view(q), view(k), view(k), view(v), view(v))
    return o.reshape(b, s, w), lse.reshape(b, s, w)


def _rec_kernel(xr_ref, gr_ref, cw_ref, cb_ref, wr_ref, br_ref, wi_ref, bi_ref, lam_ref, g_ref,
                out_ref, xe, a_s, h_s, carry):
    ts = xr_ref.shape[0]
    pad = SUBLANES

    @pl.when(pl.program_id(1) == 0)
    def _():
        xe[0:pad, :] = jnp.zeros((pad, xe.shape[1]), F32)
        carry[...] = jnp.zeros_like(carry)

    x = xr_ref[...]
    xe[pad:, :] = x
    xc = cb_ref[...] + cw_ref[REC_CONV - 1:REC_CONV, :] * x
    for kk in range(1, REC_CONV):
        xc = xc + cw_ref[REC_CONV - 1 - kk:REC_CONV - kk, :] * xe[pl.ds(pad - kk, ts), :]
    xe[0:pad, :] = x[ts - pad:, :]

    xb = xc.astype(BF16)
    r = jax.nn.sigmoid(jnp.dot(xb, wr_ref[...], preferred_element_type=F32) + br_ref[...])
    i = jax.nn.sigmoid(jnp.dot(xb, wi_ref[...], preferred_element_type=F32) + bi_ref[...])
    z = -lam_ref[...]
    softplus = jnp.maximum(z, 0.0) + jnp.log1p(jnp.exp(-jnp.abs(z)))
    log_a = (-LRU_C * softplus) * r
    a = jnp.exp(log_a)
    a_s[...] = a
    h_s[...] = jnp.sqrt(jnp.tanh(-log_a) * (a * a + 1.0)) * (i * xc)

    sub = lax.broadcasted_iota(jnp.int32, (SUBLANES, xe.shape[1]), 0)

    def group(gi, c):
        r0 = pl.multiple_of(gi * SUBLANES, SUBLANES)
        a = a_s[pl.ds(r0, SUBLANES), :]
        u = h_s[pl.ds(r0, SUBLANES), :]
        for sft in (1, 2, 4):
            keep = sub >= sft
            a_prev = jnp.where(keep, pltpu.roll(a, sft, 0), 1.0)
            u_prev = jnp.where(keep, pltpu.roll(u, sft, 0), 0.0)
            u = a * u_prev + u
            a = a * a_prev
        hh = u + a * c
        h_s[pl.ds(r0, SUBLANES), :] = hh
        return hh[SUBLANES - 1:SUBLANES, :]

    carry[...] = lax.fori_loop(0, ts // SUBLANES, group, carry[...])

    y = h_s[...] * jax.nn.gelu(gr_ref[...])
    ms = jnp.mean(y * y, axis=-1, keepdims=True)
    out_ref[...] = (y * lax.rsqrt(ms + EPS) * g_ref[...]).astype(out_ref.dtype)


def _rec(xr, gr, cw, cb, wr_bd, br, wi_bd, bi, lam, g, *, ts=512):
    b, s, w = xr.shape
    blk = pl.BlockSpec((None, ts, w), lambda bi_, t: (bi_, t, 0))
    vec = _const_spec((1, w))
    return pl.pallas_call(
        _rec_kernel,
        grid=(b, s // ts),
        in_specs=[blk, blk, _const_spec(cw.shape), vec, _const_spec(wr_bd.shape), vec,
                  _const_spec(wi_bd.shape), vec, vec, vec],
        out_specs=blk,
        out_shape=jax.ShapeDtypeStruct((b, s, w), BF16),
        scratch_shapes=[pltpu.VMEM((ts + SUBLANES, w), F32), pltpu.VMEM((ts, w), F32),
                        pltpu.VMEM((ts, w), F32), pltpu.VMEM((1, w), F32)],
        compiler_params=_cparams(("parallel", "arbitrary")),
        name="rec",
    )(xr, gr, cw, cb, wr_bd, br, wi_bd, bi, lam, g)


def _outproj_kernel(o1, o2, o3, l1, l2, l3, rec_ref, x_ref, ga_ref, wa_ref, wr_ref, out_ref):
    m = jnp.maximum(jnp.maximum(l1[...], l2[...]), l3[...])
    e1 = jnp.exp(l1[...] - m)
    e2 = jnp.exp(l2[...] - m)
    e3 = jnp.exp(l3[...] - m)
    attn = (e1 * o1[...] + e2 * o2[...] + e3 * o3[...]) / (e1 + e2 + e3)
    ms = jnp.mean(attn * attn, axis=-1, keepdims=True)
    an = (attn * lax.rsqrt(ms + EPS) * ga_ref[...]).astype(BF16)
    y = jnp.dot(an, wa_ref[...], preferred_element_type=F32)
    y = y + jnp.dot(rec_ref[...], wr_ref[...], preferred_element_type=F32)
    out_ref[...] = x_ref[...] + y


def _outproj(os_, ls_, rec, x2, ga, wa, wr, *, tm=512):
    t, d = x2.shape
    aw = wa.shape[0]
    rw = wr.shape[0]
    row = lambda w: pl.BlockSpec((tm, w), lambda i: (i, 0))
    return pl.pallas_call(
        _outproj_kernel,
        grid=(t // tm,),
        in_specs=[row(aw)] * 6 + [row(rw), row(d), _const_spec((1, aw)),
                                  _const_spec(wa.shape), _const_spec(wr.shape)],
        out_specs=row(d),
        out_shape=jax.ShapeDtypeStruct((t, d), F32),
        compiler_params=_cparams(("parallel",)),
        name="outproj",
    )(*os_, *ls_, rec, x2, ga, wa, wr)


def _ffn_kernel(x_ref, g_ref, wup_ref, cw_ref, cb_ref, wdn_ref, out_ref, ubuf, tail, acc, *, cf):
    tm = x_ref.shape[0]
    pad = SUBLANES
    nchunks = wup_ref.shape[0]

    @pl.when(pl.program_id(1) == 0)
    def _():
        tail[...] = jnp.zeros_like(tail)

    x = x_ref[...]
    ms = jnp.mean(x * x, axis=-1, keepdims=True)
    h = (x * lax.rsqrt(ms + EPS) * g_ref[...]).astype(BF16)
    acc[...] = x

    def chunk(j, carry):
        u = jnp.dot(h, wup_ref[j], preferred_element_type=F32)
        ubuf[0:pad, :] = tail[j]
        ubuf[pad:, :] = u
        cw = cw_ref[j]
        uc = cb_ref[j] + cw[FFN_CONV - 1:FFN_CONV, :] * u
        for kk in range(1, FFN_CONV):
            uc = uc + cw[FFN_CONV - 1 - kk:FFN_CONV - kk, :] * ubuf[pl.ds(pad - kk, tm), :]
        tail[j] = u[tm - pad:, :]
        act = (jax.nn.gelu(uc[:, :cf]) * uc[:, cf:]).astype(BF16)
        acc[...] += jnp.dot(act, wdn_ref[j], preferred_element_type=F32)
        return carry

    lax.fori_loop(0, nchunks, chunk, 0)
    out_ref[...] = acc[...]


def _ffn(x1, g, wup_c, cw_c, cb_c, wdn_c, *, tm=512):
    b, s, d = x1.shape
    nchunks, _, two_cf = wup_c.shape
    cf = two_cf // 2
    blk = pl.BlockSpec((None, tm, d), lambda bi, t: (bi, t, 0))
    return pl.pallas_call(
        functools.partial(_ffn_kernel, cf=cf),
        grid=(b, s // tm),
        in_specs=[blk, _const_spec((1, d)), _const_spec(wup_c.shape), _const_spec(cw_c.shape),
                  _const_spec(cb_c.shape), _const_spec(wdn_c.shape)],
        out_specs=blk,
        out_shape=jax.ShapeDtypeStruct((b, s, d), F32),
        scratch_shapes=[pltpu.VMEM((tm + SUBLANES, two_cf), F32),
                        pltpu.VMEM((nchunks, SUBLANES, two_cf), F32),
                        pltpu.VMEM((tm, d), F32)],
        compiler_params=_cparams(("parallel", "arbitrary")),
        name="ffn",
    )(x1, g, wup_c, cw_c, cb_c, wdn_c)


def _block_diag(w):
    n, c, _ = w.shape
    eye = jnp.eye(n, dtype=w.dtype)
    return (eye[:, None, :, None] * w[:, :, None, :]).reshape(n * c, n * c)


def _chunk_cols(a, d_ff, cf):
    lead = a.shape[:-1]
    g = a[..., :d_ff].reshape(*lead, d_ff // cf, cf)
    u = a[..., d_ff:].reshape(*lead, d_ff // cf, cf)
    return jnp.moveaxis(jnp.concatenate([g, u], axis=-1), -2, 0)


def kernel(x, positions, g_mix, w_in, q_norm_g, k_norm_g, rec_conv_w, rec_conv_b, w_rg, b_rg, w_ig,
           b_ig, lru_lambda, g_attn_out, g_rec_out, w_out, g_ffn, w_up, ffn_conv_w, ffn_conv_b, w_down):
    bsz, s, d = x.shape
    t = bsz * s
    depth = w_in.shape[0]
    rw = rec_conv_w.shape[-1]
    aw = w_out.shape[1] - rw
    n_heads = aw // HEAD_DIM
    d_ff = w_down.shape[1]
    cf = 512

    half = HEAD_DIM // 2
    inv_freq = ROPE_THETA ** (-jnp.arange(half, dtype=F32) / half)
    invf = jnp.tile(inv_freq, LANES // half).reshape(1, LANES)
    pos = positions.astype(F32).reshape(t, 1)
    bd = _block_diag(jnp.full((n_heads, HEAD_DIM, HEAD_DIM), 1.0 / HEAD_DIM, BF16))

    for layer in range(depth):
        q, k, v, xr, gr = _inproj(
            x.reshape(t, d), pos, g_mix[layer].reshape(1, d), w_in[layer].astype(BF16), bd,
            jnp.tile(q_norm_g[layer], n_heads).reshape(1, aw),
            jnp.tile(k_norm_g[layer], n_heads).reshape(1, aw), invf, aw=aw, rw=rw)

        os_, ls_ = [], []
        for window, dilation in WINDOW_DILATIONS:
            o, lse = _attn_branch(q.reshape(bsz, s, aw), k.reshape(bsz, s, aw), v.reshape(bsz, s, aw),
                                  window, dilation)
            os_.append(o.reshape(t, aw))
            ls_.append(lse.reshape(t, aw))

        rec = _rec(xr.reshape(bsz, s, rw), gr.reshape(bsz, s, rw), rec_conv_w[layer],
                   rec_conv_b[layer].reshape(1, rw), _block_diag(w_rg[layer]).astype(BF16),
                   b_rg[layer].reshape(1, rw), _block_diag(w_ig[layer]).astype(BF16),
                   b_ig[layer].reshape(1, rw), lru_lambda[layer].reshape(1, rw),
                   g_rec_out[layer].reshape(1, rw))

        w_out_b = w_out[layer].astype(BF16)
        x1 = _outproj(os_, ls_, rec.reshape(t, rw), x.reshape(t, d), g_attn_out[layer].reshape(1, aw),
                      w_out_b[:aw], w_out_b[aw:])

        x = _ffn(x1.reshape(bsz, s, d), g_ffn[layer].reshape(1, d),
                 _chunk_cols(w_up[layer].astype(BF16), d_ff, cf),
                 _chunk_cols(ffn_conv_w[layer], d_ff, cf),
                 _chunk_cols(ffn_conv_b[layer].reshape(1, -1), d_ff, cf),
                 w_down[layer].astype(BF16).reshape(d_ff // cf, cf, d))
    return x
```

```python
import functools

import jax
import jax.numpy as jnp
from jax import lax
from jax.experimental import pallas as pl
from jax.experimental.pallas import tpu as pltpu

HEAD_DIM = 64
REC_CONV = 4
LRU_C = 8.0
FFN_CONV = 3
WINDOW_DILATIONS = ((128, 1), (512, 4), (2048, 16))
DIL_MID, DIL_FAR = 4, 16
BLOCK = 128
ROPE_THETA = 10000.0
EPS = 1e-6
NEG_INF = -1e30

LANES = 128
SUBLANES = 8
VMEM_LIMIT = 56 * 1024 * 1024

F32 = jnp.float32
BF16 = jnp.bfloat16


def _cparams(sem):
    return pltpu.CompilerParams(dimension_semantics=sem, vmem_limit_bytes=VMEM_LIMIT)


def _const_spec(shape):
    nd = len(shape)
    return pl.BlockSpec(shape, lambda *_: (0,) * nd)


def _inproj_kernel(x_ref, pos_ref, gmix_ref, w_ref, bd_ref, qg_ref, kg_ref, invf_ref,
                   q1, k1, v1, q4, k4, v4, q16, k16, v16, xr_ref, gr_ref, s1, s4, *, aw, rw):
    x = x_ref[...]
    ms = jnp.mean(x * x, axis=-1, keepdims=True)
    h = (x * lax.rsqrt(ms + EPS) * gmix_ref[...]).astype(BF16)

    tm = x.shape[0]
    ang = pos_ref[...] * invf_ref[...]
    cos = jnp.cos(ang)
    sin = jnp.sin(ang)
    lane = lax.broadcasted_iota(jnp.int32, (tm, LANES), 1)
    first_half = (lane & (HEAD_DIM // 2)) == 0
    sin_signed = jnp.where(first_half, -sin, sin)
    nslab = aw // LANES

    def head_norm_rotary(p, g_ref):
        ms_h = jnp.dot((p * p).astype(BF16), bd_ref[...], preferred_element_type=F32)
        pn = p * lax.rsqrt(ms_h + EPS) * g_ref[...]
        outs = []
        for g in range(nslab):
            xg = pn[:, g * LANES:(g + 1) * LANES]
            swapped = jnp.where(first_half,
                                pltpu.roll(xg, LANES - HEAD_DIM // 2, 1),
                                pltpu.roll(xg, HEAD_DIM // 2, 1))
            outs.append(xg * cos + swapped * sin_signed)
        return jnp.concatenate(outs, axis=1)

    def emit(val, o1, o4, o16):
        o1[...] = val.astype(BF16)
        n4 = tm // DIL_MID
        n16 = tm // DIL_FAR
        for g in range(nslab):
            cols = slice(g * LANES, (g + 1) * LANES)
            s1[g] = val[:, cols]
            for c in range(DIL_MID):
                t4 = s1[g, pl.ds(c, n4, stride=DIL_MID), :]
                o4[c, :, cols] = t4.astype(BF16)
                s4[g, c * n4:(c + 1) * n4, :] = t4
            for c4 in range(DIL_MID):
                for cp in range(DIL_FAR // DIL_MID):
                    t16 = s4[g, pl.ds(c4 * n4 + cp, n16, stride=DIL_MID), :]
                    o16[c4 + DIL_MID * cp, :, cols] = t16.astype(BF16)

    qp = jnp.dot(h, w_ref[:, 0:aw], preferred_element_type=F32)
    emit(head_norm_rotary(qp, qg_ref) * (HEAD_DIM ** -0.5), q1, q4, q16)
    kp = jnp.dot(h, w_ref[:, aw:2 * aw], preferred_element_type=F32)
    emit(head_norm_rotary(kp, kg_ref), k1, k4, k16)
    emit(jnp.dot(h, w_ref[:, 2 * aw:3 * aw], preferred_element_type=F32), v1, v4, v16)
    xr_ref[...] = jnp.dot(h, w_ref[:, 3 * aw:3 * aw + rw], preferred_element_type=F32)
    gr_ref[...] = jnp.dot(h, w_ref[:, 3 * aw + rw:3 * aw + 2 * rw], preferred_element_type=F32)


def _inproj(x, pos, g_mix, w_in_b, bd, qg, kg, invf, *, aw, rw, tm=512):
    b, s, d = x.shape
    row = lambda w: pl.BlockSpec((None, tm, w), lambda bi, i: (bi, i, 0))
    grp = lambda dil: pl.BlockSpec((None, dil, tm // dil, aw), lambda bi, i: (bi, 0, i, 0))
    grp_shape = lambda dil: jax.ShapeDtypeStruct((b, dil, s // dil, aw), BF16)
    tok_shape = jax.ShapeDtypeStruct((b, s, aw), BF16)
    return pl.pallas_call(
        functools.partial(_inproj_kernel, aw=aw, rw=rw),
        grid=(b, s // tm),
        in_specs=[row(d), row(1), _const_spec((1, d)), _const_spec(w_in_b.shape),
                  _const_spec(bd.shape), _const_spec((1, aw)), _const_spec((1, aw)),
                  _const_spec((1, LANES))],
        out_specs=[row(aw)] * 3 + [grp(DIL_MID)] * 3 + [grp(DIL_FAR)] * 3 + [row(rw)] * 2,
        out_shape=[tok_shape] * 3 + [grp_shape(DIL_MID)] * 3 + [grp_shape(DIL_FAR)] * 3
                  + [jax.ShapeDtypeStruct((b, s, rw), F32)] * 2,
        scratch_shapes=[pltpu.VMEM((aw // LANES, tm, LANES), F32)] * 2,
        compiler_params=_cparams(("parallel", "parallel")),
        name="inproj",
    )(x, pos, g_mix, w_in_b, bd, qg, kg, invf)


GROUP_BLOCKS = 8


def _attn_kernel(q1, k1, v1, q4, k4, v4, q16, k16, v16, out_ref,
                 va1, vb1, va4, vb4, va16, vb16, acc_s, m_s, l_s, tmp_a, tmp_m, tmp_l,
                 bias_s, s_scr, p_scr, mx_scr, *, span):
    seq = out_ref.shape[0]
    grp = seq // DIL_MID
    gb = GROUP_BLOCKS
    lane = lax.broadcasted_iota(jnp.int32, (BLOCK, LANES), 1)
    low_head = lane < HEAD_DIM
    high_head = jnp.logical_not(low_head)

    qi = lax.broadcasted_iota(jnp.int32, (BLOCK, 2 * BLOCK), 0)
    kj = lax.broadcasted_iota(jnp.int32, (BLOCK, 2 * BLOCK), 1)
    for e in range(2):
        rel = qi - kj + e * BLOCK
        bias_s[e] = jnp.where((rel >= 0) & (rel <= span), 0.0, NEG_INF).astype(F32)

    for v_ref, va, vb in ((v1, va1, vb1), (v4, va4, vb4), (v16, va16, vb16)):
        for c in range(v_ref.shape[0]):
            vv = v_ref[c]
            lo = lax.broadcasted_iota(jnp.int32, vv.shape, 1) < HEAD_DIM
            one = jnp.ones_like(vv)
            va[c] = jnp.where(lo, vv, one)
            vb[c] = jnp.where(lo, one, vv)

    def rows_of(i):
        if isinstance(i, int):
            return i * BLOCK, max(i - 1, 0) * BLOCK, min(i, 1)
        return (pl.multiple_of(i * BLOCK, BLOCK),
                pl.multiple_of(jnp.maximum(i - 1, 0) * BLOCK, BLOCK), jnp.minimum(i, 1))

    def run_group(q_ref, k_ref, va, vb, coords, consume):
        for t, (c, i) in enumerate(coords):
            r0, start, e = rows_of(i)
            qp = q_ref[c, pl.ds(r0, BLOCK), :]
            ks = k_ref[c, pl.ds(start, 2 * BLOCK), :]
            for h, sel in enumerate((low_head, high_head)):
                qm = jnp.where(sel, qp, jnp.zeros_like(qp))
                s = lax.dot_general(qm, ks, (((1,), (1,)), ((), ())), preferred_element_type=F32)
                s_scr[2 * t + h] = s + bias_s[e]
        for hb in range(2 * len(coords)):
            m = jnp.max(jnp.maximum(s_scr[hb, :, 0:BLOCK], s_scr[hb, :, BLOCK:]), axis=-1, keepdims=True)
            mx_scr[hb] = jnp.broadcast_to(m, (BLOCK, LANES))
        for hb in range(2 * len(coords)):
            m = mx_scr[hb]
            p_scr[hb, :, 0:BLOCK] = jnp.exp(s_scr[hb, :, 0:BLOCK] - m).astype(BF16)
            p_scr[hb, :, BLOCK:] = jnp.exp(s_scr[hb, :, BLOCK:] - m).astype(BF16)
        for t, (c, i) in enumerate(coords):
            _, start, _ = rows_of(i)
            pva = jnp.dot(p_scr[2 * t], va[c, pl.ds(start, 2 * BLOCK), :], preferred_element_type=F32)
            pvb = jnp.dot(p_scr[2 * t + 1], vb[c, pl.ds(start, 2 * BLOCK), :], preferred_element_type=F32)
            acc = jnp.where(low_head, pva, pvb)
            l = pltpu.roll(jnp.where(low_head, pvb, pva), HEAD_DIM, 1)
            m = jnp.where(low_head, mx_scr[2 * t], mx_scr[2 * t + 1])
            consume(c, i, (acc, m, l))

    def merge(old, new):
        acc0, m0, l0 = old
        acc1, m1, l1 = new
        mn = jnp.maximum(m0, m1)
        e0 = jnp.exp(m0 - mn)
        e1 = jnp.exp(m1 - mn)
        return e0 * acc0 + e1 * acc1, mn, e0 * l0 + e1 * l1

    def for_groups(n, fn):
        def body(gi, carry):
            fn(gi)
            return carry
        lax.fori_loop(0, n, body, 0)

    def mid_store(c, i, new):
        rows = pl.ds(pl.multiple_of(c * grp + i * BLOCK, BLOCK), BLOCK)
        for ref, val in zip((acc_s, m_s, l_s), new):
            ref[rows, :] = val
    for_groups(DIL_MID, lambda gi: run_group(
        q4, k4, va4, vb4, [(gi, t) for t in range(gb)], mid_store))

    ratio = DIL_FAR // DIL_MID
    far_blocks = seq // DIL_FAR // BLOCK

    def far_merge(c, i, new):
        row0 = (c % DIL_MID) * grp + c // DIL_MID + i * (BLOCK * ratio)
        rows = pl.ds(row0, BLOCK, stride=ratio)
        merged = merge((acc_s[rows, :], m_s[rows, :], l_s[rows, :]), new)
        for ref, val in zip((acc_s, m_s, l_s), merged):
            ref[rows, :] = val
    for_groups(DIL_FAR * far_blocks // gb, lambda gi: run_group(
        q16, k16, va16, vb16,
        [(gi * (gb // far_blocks) + t // far_blocks, t % far_blocks) for t in range(gb)], far_merge))

    def near_emit(c, i, new):
        per = BLOCK // DIL_MID
        for r in range(DIL_MID):
            src = pl.ds(pl.multiple_of(r * grp + i * per, per), per)
            dst = pl.ds(r, per, stride=DIL_MID)
            tmp_a[dst, :] = acc_s[src, :]
            tmp_m[dst, :] = m_s[src, :]
            tmp_l[dst, :] = l_s[src, :]
        acc, _, l = merge((tmp_a[...], tmp_m[...], tmp_l[...]), new)
        out_ref[pl.ds(pl.multiple_of(i * BLOCK, BLOCK), BLOCK), :] = (acc / l).astype(out_ref.dtype)
    for_groups(seq // BLOCK // gb, lambda gi: run_group(
        q1, k1, va1, vb1, [(0, gi * gb + t) for t in range(gb)], near_emit))


def _attention(q1, k1, v1, q4, k4, v4, q16, k16, v16):
    b, s, w = q1.shape
    assert WINDOW_DILATIONS == ((BLOCK, 1), (BLOCK * DIL_MID, DIL_MID), (BLOCK * DIL_FAR, DIL_FAR))
    assert s % (BLOCK * DIL_FAR) == 0 and s // DIL_FAR >= 2 * BLOCK
    gb = GROUP_BLOCKS
    assert s // DIL_MID // BLOCK == gb and gb % (s // DIL_FAR // BLOCK) == 0 and (s // BLOCK) % gb == 0
    tok4 = lambda t: t.reshape(b, 1, s, w)
    spec = lambda dil: pl.BlockSpec((None, dil, s // dil, LANES), lambda bi, g: (bi, 0, 0, g))
    vshape = lambda dil: pltpu.VMEM((dil, s // dil, LANES), BF16)
    return pl.pallas_call(
        functools.partial(_attn_kernel, span=BLOCK),
        grid=(b, w // LANES),
        in_specs=[spec(1)] * 3 + [spec(DIL_MID)] * 3 + [spec(DIL_FAR)] * 3,
        out_specs=pl.BlockSpec((None, s, LANES), lambda bi, g: (bi, 0, g)),
        out_shape=jax.ShapeDtypeStruct((b, s, w), BF16),
        scratch_shapes=[vshape(1), vshape(1), vshape(DIL_MID), vshape(DIL_MID), vshape(DIL_FAR),
                        vshape(DIL_FAR)]
                       + [pltpu.VMEM((s, LANES), F32)] * 3 + [pltpu.VMEM((BLOCK, LANES), F32)] * 3
                       + [pltpu.VMEM((2, BLOCK, 2 * BLOCK), F32),
                          pltpu.VMEM((2 * gb, BLOCK, 2 * BLOCK), F32),
                          pltpu.VMEM((2 * gb, BLOCK, 2 * BLOCK), BF16),
                          pltpu.VMEM((2 * gb, BLOCK, LANES), F32)],
        compiler_params=_cparams(("parallel", "parallel")),
        name="attn",
    )(tok4(q1), tok4(k1), tok4(v1), q4, k4, v4, q16, k16, v16)


def _rec_kernel(xr_ref, gr_ref, cw_ref, cb_ref, wr_ref, br_ref, wi_ref, bi_ref, lam_ref, g_ref,
                out_ref, xe, a_s, h_s, carry):
    ts = xr_ref.shape[0]
    pad = SUBLANES

    @pl.when(pl.program_id(1) == 0)
    def _():
        xe[0:pad, :] = jnp.zeros((pad, xe.shape[1]), F32)
        carry[...] = jnp.zeros_like(carry)

    x = xr_ref[...]
    xe[pad:, :] = x
    xc = cb_ref[...] + cw_ref[REC_CONV - 1:REC_CONV, :] * x
    for kk in range(1, REC_CONV):
        xc = xc + cw_ref[REC_CONV - 1 - kk:REC_CONV - kk, :] * xe[pl.ds(pad - kk, ts), :]
    xe[0:pad, :] = x[ts - pad:, :]

    xb = xc.astype(BF16)
    r = jax.nn.sigmoid(jnp.dot(xb, wr_ref[...], preferred_element_type=F32) + br_ref[...])
    i = jax.nn.sigmoid(jnp.dot(xb, wi_ref[...], preferred_element_type=F32) + bi_ref[...])
    z = -lam_ref[...]
    softplus = jnp.maximum(z, 0.0) + jnp.log1p(jnp.exp(-jnp.abs(z)))
    log_a = (-LRU_C * softplus) * r
    a = jnp.exp(log_a)
    a_s[...] = a
    h_s[...] = jnp.sqrt(jnp.tanh(-log_a) * (a * a + 1.0)) * (i * xc)

    sub = lax.broadcasted_iota(jnp.int32, (SUBLANES, xe.shape[1]), 0)

    def group(gi, c):
        r0 = pl.multiple_of(gi * SUBLANES, SUBLANES)
        a = a_s[pl.ds(r0, SUBLANES), :]
        u = h_s[pl.ds(r0, SUBLANES), :]
        for sft in (1, 2, 4):
            keep = sub >= sft
            a_prev = jnp.where(keep, pltpu.roll(a, sft, 0), 1.0)
            u_prev = jnp.where(keep, pltpu.roll(u, sft, 0), 0.0)
            u = a * u_prev + u
            a = a * a_prev
        hh = u + a * c
        h_s[pl.ds(r0, SUBLANES), :] = hh
        return hh[SUBLANES - 1:SUBLANES, :]

    carry[...] = lax.fori_loop(0, ts // SUBLANES, group, carry[...])

    y = h_s[...] * jax.nn.gelu(gr_ref[...])
    ms = jnp.mean(y * y, axis=-1, keepdims=True)
    out_ref[...] = (y * lax.rsqrt(ms + EPS) * g_ref[...]).astype(out_ref.dtype)


def _rec(xr, gr, cw, cb, wr_bd, br, wi_bd, bi, lam, g, *, ts=512):
    b, s, w = xr.shape
    blk = pl.BlockSpec((None, ts, w), lambda bi_, t: (bi_, t, 0))
    vec = _const_spec((1, w))
    return pl.pallas_call(
        _rec_kernel,
        grid=(b, s // ts),
        in_specs=[blk, blk, _const_spec(cw.shape), vec, _const_spec(wr_bd.shape), vec,
                  _const_spec(wi_bd.shape), vec, vec, vec],
        out_specs=blk,
        out_shape=jax.ShapeDtypeStruct((b, s, w), BF16),
        scratch_shapes=[pltpu.VMEM((ts + SUBLANES, w), F32), pltpu.VMEM((ts, w), F32),
                        pltpu.VMEM((ts, w), F32), pltpu.VMEM((1, w), F32)],
        compiler_params=_cparams(("parallel", "arbitrary")),
        name="rec",
    )(xr, gr, cw, cb, wr_bd, br, wi_bd, bi, lam, g)


def _outproj_kernel(attn_ref, rec_ref, x_ref, ga_ref, wa_ref, wr_ref, out_ref):
    attn = attn_ref[...].astype(F32)
    ms = jnp.mean(attn * attn, axis=-1, keepdims=True)
    an = (attn * lax.rsqrt(ms + EPS) * ga_ref[...]).astype(BF16)
    y = jnp.dot(an, wa_ref[...], preferred_element_type=F32)
    y = y + jnp.dot(rec_ref[...], wr_ref[...], preferred_element_type=F32)
    out_ref[...] = x_ref[...] + y


def _outproj(attn, rec, x2, ga, wa, wr, *, tm=512):
    t, d = x2.shape
    aw = wa.shape[0]
    rw = wr.shape[0]
    row = lambda w: pl.BlockSpec((tm, w), lambda i: (i, 0))
    return pl.pallas_call(
        _outproj_kernel,
        grid=(t // tm,),
        in_specs=[row(aw), row(rw), row(d), _const_spec((1, aw)),
                  _const_spec(wa.shape), _const_spec(wr.shape)],
        out_specs=row(d),
        out_shape=jax.ShapeDtypeStruct((t, d), F32),
        compiler_params=_cparams(("parallel",)),
        name="outproj",
    )(attn, rec, x2, ga, wa, wr)


def _ffn_kernel(x_ref, g_ref, wup_ref, cw_ref, cb_ref, wdn_ref, out_ref, ubuf, tail, acc, *, cf):
    tm = x_ref.shape[0]
    pad = SUBLANES
    nchunks = wup_ref.shape[0]

    @pl.when(pl.program_id(1) == 0)
    def _():
        tail[...] = jnp.zeros_like(tail)

    x = x_ref[...]
    ms = jnp.mean(x * x, axis=-1, keepdims=True)
    h = (x * lax.rsqrt(ms + EPS) * g_ref[...]).astype(BF16)
    acc[...] = x

    def chunk(j, carry):
        u = jnp.dot(h, wup_ref[j], preferred_element_type=F32)
        ubuf[0:pad, :] = tail[j]
        ubuf[pad:, :] = u
        cw = cw_ref[j]
        uc = cb_ref[j] + cw[FFN_CONV - 1:FFN_CONV, :] * u
        for kk in range(1, FFN_CONV):
            uc = uc + cw[FFN_CONV - 1 - kk:FFN_CONV - kk, :] * ubuf[pl.ds(pad - kk, tm), :]
        tail[j] = u[tm - pad:, :]
        act = (jax.nn.gelu(uc[:, :cf]) * uc[:, cf:]).astype(BF16)
        acc[...] += jnp.dot(act, wdn_ref[j], preferred_element_type=F32)
        return carry

    lax.fori_loop(0, nchunks, chunk, 0)
    out_ref[...] = acc[...]


def _ffn(x1, g, wup_c, cw_c, cb_c, wdn_c, *, tm=512):
    b, s, d = x1.shape
    nchunks, _, two_cf = wup_c.shape
    cf = two_cf // 2
    blk = pl.BlockSpec((None, tm, d), lambda bi, t: (bi, t, 0))
    return pl.pallas_call(
        functools.partial(_ffn_kernel, cf=cf),
        grid=(b, s // tm),
        in_specs=[blk, _const_spec((1, d)), _const_spec(wup_c.shape), _const_spec(cw_c.shape),
                  _const_spec(cb_c.shape), _const_spec(wdn_c.shape)],
        out_specs=blk,
        out_shape=jax.ShapeDtypeStruct((b, s, d), F32),
        scratch_shapes=[pltpu.VMEM((tm + SUBLANES, two_cf), F32),
                        pltpu.VMEM((nchunks, SUBLANES, two_cf), F32),
                        pltpu.VMEM((tm, d), F32)],
        compiler_params=_cparams(("parallel", "arbitrary")),
        name="ffn",
    )(x1, g, wup_c, cw_c, cb_c, wdn_c)


def _block_diag(w):
    n, c, _ = w.shape
    eye = jnp.eye(n, dtype=w.dtype)
    return (eye[:, None, :, None] * w[:, :, None, :]).reshape(n * c, n * c)


def _chunk_cols(a, d_ff, cf):
    lead = a.shape[:-1]
    g = a[..., :d_ff].reshape(*lead, d_ff // cf, cf)
    u = a[..., d_ff:].reshape(*lead, d_ff // cf, cf)
    return jnp.moveaxis(jnp.concatenate([g, u], axis=-1), -2, 0)


def kernel(x, positions, g_mix, w_in, q_norm_g, k_norm_g, rec_conv_w, rec_conv_b, w_rg, b_rg, w_ig,
           b_ig, lru_lambda, g_attn_out, g_rec_out, w_out, g_ffn, w_up, ffn_conv_w, ffn_conv_b, w_down):
    bsz, s, d = x.shape
    t = bsz * s
    depth = w_in.shape[0]
    rw = rec_conv_w.shape[-1]
    aw = w_out.shape[1] - rw
    n_heads = aw // HEAD_DIM
    d_ff = w_down.shape[1]
    cf = 512

    half = HEAD_DIM // 2
    inv_freq = ROPE_THETA ** (-jnp.arange(half, dtype=F32) / half)
    invf = jnp.tile(inv_freq, LANES // half).reshape(1, LANES)
    pos = positions.astype(F32).reshape(bsz, s, 1)
    bd = _block_diag(jnp.full((n_heads, HEAD_DIM, HEAD_DIM), 1.0 / HEAD_DIM, BF16))

    for layer in range(depth):
        *qkv, xr, gr = _inproj(
            x, pos, g_mix[layer].reshape(1, d), w_in[layer].astype(BF16), bd,
            jnp.tile(q_norm_g[layer], n_heads).reshape(1, aw),
            jnp.tile(k_norm_g[layer], n_heads).reshape(1, aw), invf, aw=aw, rw=rw)

        attn = _attention(*qkv)

        rec = _rec(xr, gr, rec_conv_w[layer],
                   rec_conv_b[layer].reshape(1, rw), _block_diag(w_rg[layer]).astype(BF16),
                   b_rg[layer].reshape(1, rw), _block_diag(w_ig[layer]).astype(BF16),
                   b_ig[layer].reshape(1, rw), lru_lambda[layer].reshape(1, rw),
                   g_rec_out[layer].reshape(1, rw))

        w_out_b = w_out[layer].astype(BF16)
        x1 = _outproj(attn.reshape(t, aw), rec.reshape(t, rw), x.reshape(t, d),
                      g_attn_out[layer].reshape(1, aw), w_out_b[:aw], w_out_b[aw:])

        x = _ffn(x1.reshape(bsz, s, d), g_ffn[layer].reshape(1, d),
                 _chunk_cols(w_up[layer].astype(BF16), d_ff, cf),
                 _chunk_cols(ffn_conv_w[layer], d_ff, cf),
                 _chunk_cols(ffn_conv_b[layer].reshape(1, -1), d_ff, cf),
                 w_down[layer].astype(BF16).reshape(d_ff // cf, cf, d))
    return x
```

```python
import functools

import jax
import jax.numpy as jnp
import numpy as np
from jax import lax
from jax.experimental import pallas as pl
from jax.experimental.pallas import tpu as pltpu

HEAD_DIM = 64
REC_CONV = 4
LRU_C = 8.0
FFN_CONV = 3
WINDOW_DILATIONS = ((128, 1), (512, 4), (2048, 16))
DIL_MID, DIL_FAR = 4, 16
BLOCK = 128
ROPE_THETA = 10000.0
EPS = 1e-6
NEG_INF = -1e30

LANES = 128
SUBLANES = 8
VMEM_LIMIT = 56 * 1024 * 1024

F32 = jnp.float32
BF16 = jnp.bfloat16


def _cparams(sem, flags=None):
    return pltpu.CompilerParams(dimension_semantics=sem, vmem_limit_bytes=VMEM_LIMIT, flags=flags)


_GELU_C = float(np.float32(np.sqrt(2.0 / np.pi)))
_GELU_CK = _GELU_C * float(np.float32(0.044715))


def _gelu_times(g, other):
    t = jnp.tanh(g * (_GELU_C + _GELU_CK * (g * g)))
    half = (0.5 * g) * other
    return half + half * t


def _const_spec(shape):
    nd = len(shape)
    return pl.BlockSpec(shape, lambda *_: (0,) * nd)


def _inproj_kernel(x_ref, pos_ref, gmix_ref, w_ref, qg_ref, kg_ref, invf_ref,
                   q1, k1, v1, q4, k4, v4, q16, k16, v16, xr_ref, gr_ref, s1, s4, *, aw, rw):
    x = x_ref[...]
    ms = jnp.mean(x * x, axis=-1, keepdims=True)
    h = (x * lax.rsqrt(ms + EPS) * gmix_ref[...]).astype(BF16)

    tm = x.shape[0]
    ang = pos_ref[...] * invf_ref[...]
    cos = jnp.cos(ang)
    sin = jnp.sin(ang)
    lane = lax.broadcasted_iota(jnp.int32, (tm, LANES), 1)
    first_half = (lane & (HEAD_DIM // 2)) == 0
    sin_signed = jnp.where(first_half, -sin, sin)
    nslab = aw // LANES

    low_head = lane < HEAD_DIM

    def head_norm_rotary(p, g_ref):
        outs = []
        for g in range(nslab):
            cols = slice(g * LANES, (g + 1) * LANES)
            pg = p[:, cols]
            sq = pg * pg
            s_lo = jnp.sum(jnp.where(low_head, sq, 0.0), axis=-1, keepdims=True)
            s_hi = jnp.sum(jnp.where(low_head, 0.0, sq), axis=-1, keepdims=True)
            ms_h = jnp.where(low_head, s_lo, s_hi) * (1.0 / HEAD_DIM)
            xg = pg * lax.rsqrt(ms_h + EPS) * g_ref[:, cols]
            swapped = jnp.where(first_half,
                                pltpu.roll(xg, LANES - HEAD_DIM // 2, 1),
                                pltpu.roll(xg, HEAD_DIM // 2, 1))
            outs.append(xg * cos + swapped * sin_signed)
        return jnp.concatenate(outs, axis=1)

    def emit(val, o1, o4, o16):
        o1[...] = val.astype(BF16)
        n4 = tm // DIL_MID
        n16 = tm // DIL_FAR
        for g in range(nslab):
            cols = slice(g * LANES, (g + 1) * LANES)
            s1[g] = val[:, cols]
            for c in range(DIL_MID):
                t4 = s1[g, pl.ds(c, n4, stride=DIL_MID), :]
                o4[c, :, cols] = t4.astype(BF16)
                s4[g, c * n4:(c + 1) * n4, :] = t4
            for c4 in range(DIL_MID):
                for cp in range(DIL_FAR // DIL_MID):
                    t16 = s4[g, pl.ds(c4 * n4 + cp, n16, stride=DIL_MID), :]
                    o16[c4 + DIL_MID * cp, :, cols] = t16.astype(BF16)

    qp = jnp.dot(h, w_ref[:, 0:aw], preferred_element_type=F32)
    emit(head_norm_rotary(qp, qg_ref) * (HEAD_DIM ** -0.5), q1, q4, q16)
    kp = jnp.dot(h, w_ref[:, aw:2 * aw], preferred_element_type=F32)
    emit(head_norm_rotary(kp, kg_ref), k1, k4, k16)
    emit(jnp.dot(h, w_ref[:, 2 * aw:3 * aw], preferred_element_type=F32), v1, v4, v16)
    xr_ref[...] = jnp.dot(h, w_ref[:, 3 * aw:3 * aw + rw], preferred_element_type=F32)
    gr_ref[...] = jnp.dot(h, w_ref[:, 3 * aw + rw:3 * aw + 2 * rw], preferred_element_type=F32)


def _inproj(x, pos, g_mix, w_in_b, qg, kg, invf, *, aw, rw, tm=512):
    b, s, d = x.shape
    row = lambda w: pl.BlockSpec((None, tm, w), lambda bi, i: (bi, i, 0))
    grp = lambda dil: pl.BlockSpec((None, dil, tm // dil, aw), lambda bi, i: (bi, 0, i, 0))
    grp_shape = lambda dil: jax.ShapeDtypeStruct((b, dil, s // dil, aw), BF16)
    tok_shape = jax.ShapeDtypeStruct((b, s, aw), BF16)
    return pl.pallas_call(
        functools.partial(_inproj_kernel, aw=aw, rw=rw),
        grid=(b, s // tm),
        in_specs=[row(d), row(1), _const_spec((1, d)), _const_spec(w_in_b.shape),
                  _const_spec((1, aw)), _const_spec((1, aw)),
                  _const_spec((1, LANES))],
        out_specs=[row(aw)] * 3 + [grp(DIL_MID)] * 3 + [grp(DIL_FAR)] * 3 + [row(rw)] * 2,
        out_shape=[tok_shape] * 3 + [grp_shape(DIL_MID)] * 3 + [grp_shape(DIL_FAR)] * 3
                  + [jax.ShapeDtypeStruct((b, s, rw), F32)] * 2,
        scratch_shapes=[pltpu.VMEM((aw // LANES, tm, LANES), F32)] * 2,
        compiler_params=_cparams(("parallel", "parallel")),
        name="inproj",
    )(x, pos, g_mix, w_in_b, qg, kg, invf)


GROUP_BLOCKS = 8


def _attn_kernel(q1, k1, v1, q4, k4, v4, q16, k16, v16, out_ref,
                 va1, vb1, va4, vb4, va16, vb16, acc_s, m_s, l_s, tmp_a, tmp_m, tmp_l,
                 bias_s, s_scr, p_scr, mx_scr, *, span):
    seq = out_ref.shape[0]
    grp = seq // DIL_MID
    gb = GROUP_BLOCKS
    lane = lax.broadcasted_iota(jnp.int32, (BLOCK, LANES), 1)
    low_head = lane < HEAD_DIM
    high_head = jnp.logical_not(low_head)

    qi = lax.broadcasted_iota(jnp.int32, (BLOCK, 2 * BLOCK), 0)
    kj = lax.broadcasted_iota(jnp.int32, (BLOCK, 2 * BLOCK), 1)
    for e in range(2):
        rel = qi - kj + e * BLOCK
        bias_s[e] = jnp.where((rel >= 0) & (rel <= span), 0.0, NEG_INF).astype(F32)

    for v_ref, va, vb in ((v1, va1, vb1), (v4, va4, vb4), (v16, va16, vb16)):
        for c in range(v_ref.shape[0]):
            vv = v_ref[c]
            lo = lax.broadcasted_iota(jnp.int32, vv.shape, 1) < HEAD_DIM
            one = jnp.ones_like(vv)
            va[c] = jnp.where(lo, vv, one)
            vb[c] = jnp.where(lo, one, vv)

    def rows_of(i):
        if isinstance(i, int):
            return i * BLOCK, max(i - 1, 0) * BLOCK, min(i, 1)
        return (pl.multiple_of(i * BLOCK, BLOCK),
                pl.multiple_of(jnp.maximum(i - 1, 0) * BLOCK, BLOCK), jnp.minimum(i, 1))

    def run_group(q_ref, k_ref, va, vb, coords, consume):
        for t, (c, i) in enumerate(coords):
            r0, start, e = rows_of(i)
            qp = q_ref[c, pl.ds(r0, BLOCK), :]
            ks = k_ref[c, pl.ds(start, 2 * BLOCK), :]
            for h, sel in enumerate((low_head, high_head)):
                qm = jnp.where(sel, qp, jnp.zeros_like(qp))
                s = lax.dot_general(qm, ks, (((1,), (1,)), ((), ())), preferred_element_type=F32)
                s_scr[2 * t + h] = s + bias_s[e]
        for hb in range(2 * len(coords)):
            m = jnp.max(jnp.maximum(s_scr[hb, :, 0:BLOCK], s_scr[hb, :, BLOCK:]), axis=-1, keepdims=True)
            mx_scr[hb] = jnp.broadcast_to(m, (BLOCK, LANES))
        for hb in range(2 * len(coords)):
            m = mx_scr[hb]
            p_scr[hb, :, 0:BLOCK] = jnp.exp(s_scr[hb, :, 0:BLOCK] - m).astype(BF16)
            p_scr[hb, :, BLOCK:] = jnp.exp(s_scr[hb, :, BLOCK:] - m).astype(BF16)
        for t, (c, i) in enumerate(coords):
            _, start, _ = rows_of(i)
            pva = jnp.dot(p_scr[2 * t], va[c, pl.ds(start, 2 * BLOCK), :], preferred_element_type=F32)
            pvb = jnp.dot(p_scr[2 * t + 1], vb[c, pl.ds(start, 2 * BLOCK), :], preferred_element_type=F32)
            acc = jnp.where(low_head, pva, pvb)
            l = pltpu.roll(jnp.where(low_head, pvb, pva), HEAD_DIM, 1)
            m = jnp.where(low_head, mx_scr[2 * t], mx_scr[2 * t + 1])
            consume(c, i, (acc, m, l))

    def merge(old, new):
        acc0, m0, l0 = old
        acc1, m1, l1 = new
        mn = jnp.maximum(m0, m1)
        e0 = jnp.exp(m0 - mn)
        e1 = jnp.exp(m1 - mn)
        return e0 * acc0 + e1 * acc1, mn, e0 * l0 + e1 * l1

    def for_groups(n, fn):
        def body(gi, carry):
            fn(gi)
            return carry
        lax.fori_loop(0, n, body, 0)

    def mid_store(c, i, new):
        rows = pl.ds(pl.multiple_of(c * grp + i * BLOCK, BLOCK), BLOCK)
        for ref, val in zip((acc_s, m_s, l_s), new):
            ref[rows, :] = val
    for_groups(DIL_MID, lambda gi: run_group(
        q4, k4, va4, vb4, [(gi, t) for t in range(gb)], mid_store))

    ratio = DIL_FAR // DIL_MID
    far_blocks = seq // DIL_FAR // BLOCK

    def far_merge(c, i, new):
        row0 = (c % DIL_MID) * grp + c // DIL_MID + i * (BLOCK * ratio)
        rows = pl.ds(row0, BLOCK, stride=ratio)
        merged = merge((acc_s[rows, :], m_s[rows, :], l_s[rows, :]), new)
        for ref, val in zip((acc_s, m_s, l_s), merged):
            ref[rows, :] = val
    for_groups(DIL_FAR * far_blocks // gb, lambda gi: run_group(
        q16, k16, va16, vb16,
        [(gi * (gb // far_blocks) + t // far_blocks, t % far_blocks) for t in range(gb)], far_merge))

    def near_emit(c, i, new):
        per = BLOCK // DIL_MID
        for r in range(DIL_MID):
            src = pl.ds(pl.multiple_of(r * grp + i * per, per), per)
            dst = pl.ds(r, per, stride=DIL_MID)
            tmp_a[dst, :] = acc_s[src, :]
            tmp_m[dst, :] = m_s[src, :]
            tmp_l[dst, :] = l_s[src, :]
        acc, _, l = merge((tmp_a[...], tmp_m[...], tmp_l[...]), new)
        out_ref[pl.ds(pl.multiple_of(i * BLOCK, BLOCK), BLOCK), :] = (acc / l).astype(out_ref.dtype)
    for_groups(seq // BLOCK // gb, lambda gi: run_group(
        q1, k1, va1, vb1, [(0, gi * gb + t) for t in range(gb)], near_emit))


def _attention(q1, k1, v1, q4, k4, v4, q16, k16, v16):
    b, s, w = q1.shape
    assert WINDOW_DILATIONS == ((BLOCK, 1), (BLOCK * DIL_MID, DIL_MID), (BLOCK * DIL_FAR, DIL_FAR))
    assert s % (BLOCK * DIL_FAR) == 0 and s // DIL_FAR >= 2 * BLOCK
    gb = GROUP_BLOCKS
    assert s // DIL_MID // BLOCK == gb and gb % (s // DIL_FAR // BLOCK) == 0 and (s // BLOCK) % gb == 0
    tok4 = lambda t: t.reshape(b, 1, s, w)
    spec = lambda dil: pl.BlockSpec((None, dil, s // dil, LANES), lambda bi, g: (bi, 0, 0, g))
    vshape = lambda dil: pltpu.VMEM((dil, s // dil, LANES), BF16)
    return pl.pallas_call(
        functools.partial(_attn_kernel, span=BLOCK),
        grid=(b, w // LANES),
        in_specs=[spec(1)] * 3 + [spec(DIL_MID)] * 3 + [spec(DIL_FAR)] * 3,
        out_specs=pl.BlockSpec((None, s, LANES), lambda bi, g: (bi, 0, g)),
        out_shape=jax.ShapeDtypeStruct((b, s, w), BF16),
        scratch_shapes=[vshape(1), vshape(1), vshape(DIL_MID), vshape(DIL_MID), vshape(DIL_FAR),
                        vshape(DIL_FAR)]
                       + [pltpu.VMEM((s, LANES), F32)] * 3 + [pltpu.VMEM((BLOCK, LANES), F32)] * 3
                       + [pltpu.VMEM((2, BLOCK, 2 * BLOCK), F32),
                          pltpu.VMEM((2 * gb, BLOCK, 2 * BLOCK), F32),
                          pltpu.VMEM((2 * gb, BLOCK, 2 * BLOCK), BF16),
                          pltpu.VMEM((2 * gb, BLOCK, LANES), F32)],
        compiler_params=_cparams(("parallel", "parallel")),
        name="attn",
    )(tok4(q1), tok4(k1), tok4(v1), q4, k4, v4, q16, k16, v16)


def _rec_kernel(xr_ref, gr_ref, cw_ref, cb_ref, wr_ref, br_ref, wi_ref, bi_ref, lam_ref, g_ref,
                out_ref, xe, a_s, h_s, carry):
    ts = xr_ref.shape[0]
    pad = SUBLANES

    @pl.when(pl.program_id(1) == 0)
    def _():
        xe[0:pad, :] = jnp.zeros((pad, xe.shape[1]), F32)
        carry[...] = jnp.zeros_like(carry)

    x = xr_ref[...]
    xe[pad:, :] = x
    xc = cb_ref[...] + cw_ref[REC_CONV - 1:REC_CONV, :] * x
    for kk in range(1, REC_CONV):
        xc = xc + cw_ref[REC_CONV - 1 - kk:REC_CONV - kk, :] * xe[pl.ds(pad - kk, ts), :]
    xe[0:pad, :] = x[ts - pad:, :]

    xb = xc.astype(BF16)
    r = jax.nn.sigmoid(jnp.dot(xb, wr_ref[...], preferred_element_type=F32) + br_ref[...])
    i = jax.nn.sigmoid(jnp.dot(xb, wi_ref[...], preferred_element_type=F32) + bi_ref[...])
    z = -lam_ref[...]
    softplus = jnp.maximum(z, 0.0) + jnp.log1p(jnp.exp(-jnp.abs(z)))
    log_a = (-LRU_C * softplus) * r
    a = jnp.exp(log_a)
    a_s[...] = a
    w = jnp.tanh(-log_a) * (a * a + 1.0)
    h_s[...] = jnp.where(w > 0.0, w * lax.rsqrt(w), 0.0) * (i * xc)

    sub = lax.broadcasted_iota(jnp.int32, (SUBLANES, xe.shape[1]), 0)

    def group(gi, c):
        r0 = pl.multiple_of(gi * SUBLANES, SUBLANES)
        a = a_s[pl.ds(r0, SUBLANES), :]
        u = h_s[pl.ds(r0, SUBLANES), :]
        for sft in (1, 2, 4):
            keep = sub >= sft
            a_prev = jnp.where(keep, pltpu.roll(a, sft, 0), 1.0)
            u_prev = jnp.where(keep, pltpu.roll(u, sft, 0), 0.0)
            u = a * u_prev + u
            a = a * a_prev
        hh = u + a * c
        h_s[pl.ds(r0, SUBLANES), :] = hh
        return hh[SUBLANES - 1:SUBLANES, :]

    carry[...] = lax.fori_loop(0, ts // SUBLANES, group, carry[...])

    y = _gelu_times(gr_ref[...], h_s[...])
    ms = jnp.mean(y * y, axis=-1, keepdims=True)
    out_ref[...] = (y * lax.rsqrt(ms + EPS) * g_ref[...]).astype(out_ref.dtype)


def _rec(xr, gr, cw, cb, wr_bd, br, wi_bd, bi, lam, g, *, ts=512):
    b, s, w = xr.shape
    blk = pl.BlockSpec((None, ts, w), lambda bi_, t: (bi_, t, 0))
    vec = _const_spec((1, w))
    return pl.pallas_call(
        _rec_kernel,
        grid=(b, s // ts),
        in_specs=[blk, blk, _const_spec(cw.shape), vec, _const_spec(wr_bd.shape), vec,
                  _const_spec(wi_bd.shape), vec, vec, vec],
        out_specs=blk,
        out_shape=jax.ShapeDtypeStruct((b, s, w), BF16),
        scratch_shapes=[pltpu.VMEM((ts + SUBLANES, w), F32), pltpu.VMEM((ts, w), F32),
                        pltpu.VMEM((ts, w), F32), pltpu.VMEM((1, w), F32)],
        compiler_params=_cparams(("parallel", "arbitrary")),
        name="rec",
    )(xr, gr, cw, cb, wr_bd, br, wi_bd, bi, lam, g)


def _outproj_kernel(attn_ref, rec_ref, x_ref, ga_ref, wa_ref, wr_ref, out_ref):
    attn = attn_ref[...].astype(F32)
    ms = jnp.mean(attn * attn, axis=-1, keepdims=True)
    an = (attn * lax.rsqrt(ms + EPS) * ga_ref[...]).astype(BF16)
    y = jnp.dot(an, wa_ref[...], preferred_element_type=F32)
    y = y + jnp.dot(rec_ref[...], wr_ref[...], preferred_element_type=F32)
    out_ref[...] = x_ref[...] + y


def _outproj(attn, rec, x2, ga, wa, wr, *, tm=512):
    t, d = x2.shape
    aw = wa.shape[0]
    rw = wr.shape[0]
    row = lambda w: pl.BlockSpec((tm, w), lambda i: (i, 0))
    return pl.pallas_call(
        _outproj_kernel,
        grid=(t // tm,),
        in_specs=[row(aw), row(rw), row(d), _const_spec((1, aw)),
                  _const_spec(wa.shape), _const_spec(wr.shape)],
        out_specs=row(d),
        out_shape=jax.ShapeDtypeStruct((t, d), F32),
        compiler_params=_cparams(("parallel",)),
        name="outproj",
    )(attn, rec, x2, ga, wa, wr)


def _ffn_kernel(x_ref, g_ref, wup_ref, cw_ref, cb_ref, wdn_ref, out_ref, ua, ub, tail, acc, h_s, act_a, act_b,
                *, cf):
    tm = x_ref.shape[0]
    pad = SUBLANES
    d_ff = wdn_ref.shape[0]
    nchunks = d_ff // cf
    ubufs = (ua, ub)
    acts = (act_a, act_b)

    @pl.when(pl.program_id(1) == 0)
    def _():
        tail[...] = jnp.zeros_like(tail)
        for buf in ubufs:
            buf[:, pad + tm:, :] = jnp.zeros((FFN_CONV, pad, buf.shape[2]), F32)

    x = x_ref[...]
    ms = jnp.mean(x * x, axis=-1, keepdims=True)
    h_s[...] = (x * lax.rsqrt(ms + EPS) * g_ref[...]).astype(BF16)
    acc[...] = x

    def halves(j):
        for half in range(2):
            yield (slice(half * d_ff + j * cf, half * d_ff + (j + 1) * cf),
                   slice(half * cf, (half + 1) * cf))

    def up(j):
        buf = ubufs[j % 2]
        for cols, dst in halves(j):
            u = jnp.dot(h_s[...], wup_ref[:, cols], preferred_element_type=F32)
            buf[0, pad:pad + tm, dst] = u
            for kk in range(1, FFN_CONV):
                buf[kk, pad:2 * pad, dst] = tail[kk - 1, :, cols]
                buf[kk, pad + kk:pad + kk + tm, dst] = u
                tail[kk - 1, :, cols] = buf[kk, pad + tm:2 * pad + tm, dst]

    def elementwise(j):
        buf = ubufs[j % 2]
        parts = []
        for cols, dst in halves(j):
            uc = cb_ref[:, cols]
            for kk in range(FFN_CONV):
                uc = uc + cw_ref[FFN_CONV - 1 - kk:FFN_CONV - kk, cols] * buf[kk, pad:pad + tm, dst]
            parts.append(uc)
        acts[j % 2][...] = _gelu_times(parts[0], parts[1]).astype(BF16)

    def down(j):
        acc[...] += jnp.dot(acts[j % 2][...], wdn_ref[j * cf:(j + 1) * cf, :], preferred_element_type=F32)

    up(0)
    for j in range(nchunks + 1):
        if j + 1 < nchunks:
            up(j + 1)
        if j < nchunks:
            elementwise(j)
        if j >= 1:
            down(j - 1)
    out_ref[...] = acc[...]


def _ffn(x1, g, wup_b, cw, cb, wdn_b, *, tm=512, cf=512):
    b, s, d = x1.shape
    d_ff = wdn_b.shape[0]
    assert d_ff % cf == 0
    blk = pl.BlockSpec((None, tm, d), lambda bi, t: (bi, t, 0))
    return pl.pallas_call(
        functools.partial(_ffn_kernel, cf=cf),
        grid=(b, s // tm),
        in_specs=[blk, _const_spec((1, d)), _const_spec(wup_b.shape), _const_spec(cw.shape),
                  _const_spec(cb.shape), _const_spec(wdn_b.shape)],
        out_specs=blk,
        out_shape=jax.ShapeDtypeStruct((b, s, d), F32),
        scratch_shapes=[pltpu.VMEM((FFN_CONV, tm + 2 * SUBLANES, 2 * cf), F32)] * 2
                       + [pltpu.VMEM((FFN_CONV - 1, SUBLANES, 2 * d_ff), F32), pltpu.VMEM((tm, d), F32),
                          pltpu.VMEM((tm, d), BF16), pltpu.VMEM((tm, cf), BF16),
                          pltpu.VMEM((tm, cf), BF16)],
        compiler_params=_cparams(("parallel", "arbitrary")),
        name="ffn",
    )(x1, g, wup_b, cw, cb, wdn_b)


def _block_diag(w):
    n, c, _ = w.shape
    eye = jnp.eye(n, dtype=w.dtype)
    return (eye[:, None, :, None] * w[:, :, None, :]).reshape(n * c, n * c)


def kernel(x, positions, g_mix, w_in, q_norm_g, k_norm_g, rec_conv_w, rec_conv_b, w_rg, b_rg, w_ig,
           b_ig, lru_lambda, g_attn_out, g_rec_out, w_out, g_ffn, w_up, ffn_conv_w, ffn_conv_b, w_down):
    bsz, s, d = x.shape
    t = bsz * s
    depth = w_in.shape[0]
    rw = rec_conv_w.shape[-1]
    aw = w_out.shape[1] - rw
    n_heads = aw // HEAD_DIM

    half = HEAD_DIM // 2
    inv_freq = ROPE_THETA ** (-jnp.arange(half, dtype=F32) / half)
    invf = jnp.tile(inv_freq, LANES // half).reshape(1, LANES)
    pos = positions.astype(F32).reshape(bsz, s, 1)

    for layer in range(depth):
        *qkv, xr, gr = _inproj(
            x, pos, g_mix[layer].reshape(1, d), w_in[layer].astype(BF16),
            jnp.tile(q_norm_g[layer], n_heads).reshape(1, aw),
            jnp.tile(k_norm_g[layer], n_heads).reshape(1, aw), invf, aw=aw, rw=rw)

        attn = _attention(*qkv)

        rec = _rec(xr, gr, rec_conv_w[layer],
                   rec_conv_b[layer].reshape(1, rw), _block_diag(w_rg[layer]).astype(BF16),
                   b_rg[layer].reshape(1, rw), _block_diag(w_ig[layer]).astype(BF16),
                   b_ig[layer].reshape(1, rw), lru_lambda[layer].reshape(1, rw),
                   g_rec_out[layer].reshape(1, rw))

        w_out_b = w_out[layer].astype(BF16)
        x1 = _outproj(attn.reshape(t, aw), rec.reshape(t, rw), x.reshape(t, d),
                      g_attn_out[layer].reshape(1, aw), w_out_b[:aw], w_out_b[aw:])

        x = _ffn(x1.reshape(bsz, s, d), g_ffn[layer].reshape(1, d), w_up[layer].astype(BF16),
                 ffn_conv_w[layer], ffn_conv_b[layer].reshape(1, -1), w_down[layer].astype(BF16))
    return x
```

```python
import functools

import jax
import jax.numpy as jnp
import numpy as np
from jax import lax
from jax.experimental import pallas as pl
from jax.experimental.pallas import tpu as pltpu

HEAD_DIM = 64
REC_CONV = 4
LRU_C = 8.0
FFN_CONV = 3
WINDOW_DILATIONS = ((128, 1), (512, 4), (2048, 16))
DIL_MID, DIL_FAR = 4, 16
BLOCK = 128
ROPE_THETA = 10000.0
EPS = 1e-6
NEG_INF = -1e30

LANES = 128
SUBLANES = 8
VMEM_LIMIT = 56 * 1024 * 1024

F32 = jnp.float32
BF16 = jnp.bfloat16


def _cparams(sem, flags=None):
    return pltpu.CompilerParams(dimension_semantics=sem, vmem_limit_bytes=VMEM_LIMIT, flags=flags)


_GELU_C = float(np.float32(np.sqrt(2.0 / np.pi)))
_GELU_CK = _GELU_C * float(np.float32(0.044715))


def _gelu_times(g, other):
    t = jnp.tanh(g * (_GELU_C + _GELU_CK * (g * g)))
    half = (0.5 * g) * other
    return half + half * t


def _const_spec(shape):
    nd = len(shape)
    return pl.BlockSpec(shape, lambda *_: (0,) * nd)


def _inproj_kernel(x_ref, pos_ref, gmix_ref, w_ref, qg_ref, kg_ref, invf_ref,
                   q1, k1, v1, q4, k4, v4, q16, k16, v16, xr_ref, gr_ref, s1, s4, *, aw, rw, nsub):
    tm = x_ref.shape[0]
    rows_per = tm // nsub
    nslab = aw // LANES
    lane = lax.broadcasted_iota(jnp.int32, (rows_per, LANES), 1)
    head_a = (lane & (HEAD_DIM // 2)) == 0
    n4 = rows_per // DIL_MID
    n16 = rows_per // DIL_FAR

    for sub in range(nsub):
        rows = slice(sub * rows_per, (sub + 1) * rows_per)
        x = x_ref[rows, :]
        ms = jnp.mean(x * x, axis=-1, keepdims=True)
        h = (x * lax.rsqrt(ms + EPS) * gmix_ref[...]).astype(BF16)

        ang = pos_ref[rows, :] * invf_ref[...]
        cos = jnp.cos(ang)
        sin = jnp.sin(ang)
        sin_signed = jnp.where(lane < HEAD_DIM, -sin, sin)

        def head_norm_rotary(p, g_ref):
            outs = []
            for g in range(nslab):
                cols = slice(g * LANES, (g + 1) * LANES)
                pg = p[:, cols]
                sq = pg * pg
                s_a = jnp.sum(jnp.where(head_a, sq, 0.0), axis=-1, keepdims=True)
                s_b = jnp.sum(jnp.where(head_a, 0.0, sq), axis=-1, keepdims=True)
                ms_h = jnp.where(head_a, s_a, s_b) * (1.0 / HEAD_DIM)
                xg = pg * lax.rsqrt(ms_h + EPS) * g_ref[:, cols]
                outs.append(xg * cos + pltpu.roll(xg, HEAD_DIM, 1) * sin_signed)
            return jnp.concatenate(outs, axis=1)

        def emit(which, val, o1, o4, o16):
            o1[rows, :] = val.astype(BF16)
            base = sub * rows_per
            for g in range(nslab):
                cols = slice(g * LANES, (g + 1) * LANES)
                s1[which, g, rows, :] = val[:, cols]
                for c in range(DIL_MID):
                    t4 = s1[which, g, pl.ds(base + c, n4, stride=DIL_MID), :]
                    o4[c, sub * n4:(sub + 1) * n4, cols] = t4.astype(BF16)
                    s4[which, g, base + c * n4:base + (c + 1) * n4, :] = t4
                for c4 in range(DIL_MID):
                    for cp in range(DIL_FAR // DIL_MID):
                        t16 = s4[which, g, pl.ds(base + c4 * n4 + cp, n16, stride=DIL_MID), :]
                        o16[c4 + DIL_MID * cp, sub * n16:(sub + 1) * n16, cols] = t16.astype(BF16)

        qp = jnp.dot(h, w_ref[:, 0:aw], preferred_element_type=F32)
        kp = jnp.dot(h, w_ref[:, aw:2 * aw], preferred_element_type=F32)
        vp = jnp.dot(h, w_ref[:, 2 * aw:3 * aw], preferred_element_type=F32)
        emit(0, head_norm_rotary(qp, qg_ref) * (HEAD_DIM ** -0.5), q1, q4, q16)
        xr_ref[rows, :] = jnp.dot(h, w_ref[:, 3 * aw:3 * aw + rw], preferred_element_type=F32)
        emit(1, head_norm_rotary(kp, kg_ref), k1, k4, k16)
        gr_ref[rows, :] = jnp.dot(h, w_ref[:, 3 * aw + rw:3 * aw + 2 * rw], preferred_element_type=F32)
        emit(2, vp, v1, v4, v16)


def _inproj(x, pos, g_mix, w_in_b, qg, kg, invf, *, aw, rw, tm=512):
    b, s, d = x.shape
    row = lambda w: pl.BlockSpec((None, tm, w), lambda bi, i: (bi, i, 0))
    grp = lambda dil: pl.BlockSpec((None, dil, tm // dil, aw), lambda bi, i: (bi, 0, i, 0))
    grp_shape = lambda dil: jax.ShapeDtypeStruct((b, dil, s // dil, aw), BF16)
    tok_shape = jax.ShapeDtypeStruct((b, s, aw), BF16)
    return pl.pallas_call(
        functools.partial(_inproj_kernel, aw=aw, rw=rw, nsub=2),
        grid=(b, s // tm),
        in_specs=[row(d), row(1), _const_spec((1, d)), _const_spec(w_in_b.shape),
                  _const_spec((1, aw)), _const_spec((1, aw)),
                  _const_spec((1, LANES))],
        out_specs=[row(aw)] * 3 + [grp(DIL_MID)] * 3 + [grp(DIL_FAR)] * 3 + [row(rw)] * 2,
        out_shape=[tok_shape] * 3 + [grp_shape(DIL_MID)] * 3 + [grp_shape(DIL_FAR)] * 3
                  + [jax.ShapeDtypeStruct((b, s, rw), F32)] * 2,
        scratch_shapes=[pltpu.VMEM((3, aw // LANES, tm, LANES), F32)] * 2,
        compiler_params=_cparams(("parallel", "parallel")),
        name="inproj",
    )(x, pos, g_mix, w_in_b, qg, kg, invf)


GROUP_BLOCKS = 8


def _attn_kernel(q1, k1, v1, q4, k4, v4, q16, k16, v16, out_ref,
                 va1, vb1, va4, vb4, va16, vb16, acc_s, m_s, l_s, tmp_a, tmp_m, tmp_l,
                 bias_s, s_scr, p_scr, mx_scr, *, span):
    seq = out_ref.shape[0]
    grp = seq // DIL_MID
    gb = GROUP_BLOCKS
    lane = lax.broadcasted_iota(jnp.int32, (BLOCK, LANES), 1)
    low_head = lane < HEAD_DIM
    q_head_a = (lane & (HEAD_DIM // 2)) == 0
    q_head_b = jnp.logical_not(q_head_a)

    qi = lax.broadcasted_iota(jnp.int32, (BLOCK, 2 * BLOCK), 0)
    kj = lax.broadcasted_iota(jnp.int32, (BLOCK, 2 * BLOCK), 1)
    for e in range(2):
        rel = qi - kj + e * BLOCK
        bias_s[e] = jnp.where((rel >= 0) & (rel <= span), 0.0, NEG_INF).astype(F32)

    for v_ref, va, vb in ((v1, va1, vb1), (v4, va4, vb4), (v16, va16, vb16)):
        for c in range(v_ref.shape[0]):
            vv = v_ref[c]
            lo = lax.broadcasted_iota(jnp.int32, vv.shape, 1) < HEAD_DIM
            one = jnp.ones_like(vv)
            va[c] = jnp.where(lo, vv, one)
            vb[c] = jnp.where(lo, one, vv)

    def rows_of(i):
        if isinstance(i, int):
            return i * BLOCK, max(i - 1, 0) * BLOCK, min(i, 1)
        return (pl.multiple_of(i * BLOCK, BLOCK),
                pl.multiple_of(jnp.maximum(i - 1, 0) * BLOCK, BLOCK), jnp.minimum(i, 1))

    def run_group(q_ref, k_ref, va, vb, coords, consume):
        for t, (c, i) in enumerate(coords):
            r0, start, e = rows_of(i)
            qp = q_ref[c, pl.ds(r0, BLOCK), :]
            ks = k_ref[c, pl.ds(start, 2 * BLOCK), :]
            for h, sel in enumerate((q_head_a, q_head_b)):
                qm = jnp.where(sel, qp, jnp.zeros_like(qp))
                s = lax.dot_general(qm, ks, (((1,), (1,)), ((), ())), preferred_element_type=F32)
                s_scr[2 * t + h] = s + bias_s[e]
        for hb in range(2 * len(coords)):
            m = jnp.max(jnp.maximum(s_scr[hb, :, 0:BLOCK], s_scr[hb, :, BLOCK:]), axis=-1, keepdims=True)
            mx_scr[hb] = jnp.broadcast_to(m, (BLOCK, LANES))
        for hb in range(2 * len(coords)):
            m = mx_scr[hb]
            p_scr[hb, :, 0:BLOCK] = jnp.exp(s_scr[hb, :, 0:BLOCK] - m).astype(BF16)
            p_scr[hb, :, BLOCK:] = jnp.exp(s_scr[hb, :, BLOCK:] - m).astype(BF16)
        for t, (c, i) in enumerate(coords):
            _, start, _ = rows_of(i)
            pva = jnp.dot(p_scr[2 * t], va[c, pl.ds(start, 2 * BLOCK), :], preferred_element_type=F32)
            pvb = jnp.dot(p_scr[2 * t + 1], vb[c, pl.ds(start, 2 * BLOCK), :], preferred_element_type=F32)
            acc = jnp.where(low_head, pva, pvb)
            l = pltpu.roll(jnp.where(low_head, pvb, pva), HEAD_DIM, 1)
            m = jnp.where(low_head, mx_scr[2 * t], mx_scr[2 * t + 1])
            consume(c, i, (acc, m, l))

    def merge(old, new):
        acc0, m0, l0 = old
        acc1, m1, l1 = new
        mn = jnp.maximum(m0, m1)
        e0 = jnp.exp(m0 - mn)
        e1 = jnp.exp(m1 - mn)
        return e0 * acc0 + e1 * acc1, mn, e0 * l0 + e1 * l1

    def for_groups(n, fn):
        def body(gi, carry):
            fn(gi)
            return carry
        lax.fori_loop(0, n, body, 0)

    def mid_store(c, i, new):
        rows = pl.ds(pl.multiple_of(c * grp + i * BLOCK, BLOCK), BLOCK)
        for ref, val in zip((acc_s, m_s, l_s), new):
            ref[rows, :] = val
    for_groups(DIL_MID, lambda gi: run_group(
        q4, k4, va4, vb4, [(gi, t) for t in range(gb)], mid_store))

    ratio = DIL_FAR // DIL_MID
    far_blocks = seq // DIL_FAR // BLOCK

    def far_merge(c, i, new):
        row0 = (c % DIL_MID) * grp + c // DIL_MID + i * (BLOCK * ratio)
        rows = pl.ds(row0, BLOCK, stride=ratio)
        merged = merge((acc_s[rows, :], m_s[rows, :], l_s[rows, :]), new)
        for ref, val in zip((acc_s, m_s, l_s), merged):
            ref[rows, :] = val
    for_groups(DIL_FAR * far_blocks // gb, lambda gi: run_group(
        q16, k16, va16, vb16,
        [(gi * (gb // far_blocks) + t // far_blocks, t % far_blocks) for t in range(gb)], far_merge))

    def near_emit(c, i, new):
        per = BLOCK // DIL_MID
        for r in range(DIL_MID):
            src = pl.ds(pl.multiple_of(r * grp + i * per, per), per)
            dst = pl.ds(r, per, stride=DIL_MID)
            tmp_a[dst, :] = acc_s[src, :]
            tmp_m[dst, :] = m_s[src, :]
            tmp_l[dst, :] = l_s[src, :]
        acc, _, l = merge((tmp_a[...], tmp_m[...], tmp_l[...]), new)
        out_ref[pl.ds(pl.multiple_of(i * BLOCK, BLOCK), BLOCK), :] = (acc / l).astype(out_ref.dtype)
    for_groups(seq // BLOCK // gb, lambda gi: run_group(
        q1, k1, va1, vb1, [(0, gi * gb + t) for t in range(gb)], near_emit))


def _attention(q1, k1, v1, q4, k4, v4, q16, k16, v16):
    b, s, w = q1.shape
    assert WINDOW_DILATIONS == ((BLOCK, 1), (BLOCK * DIL_MID, DIL_MID), (BLOCK * DIL_FAR, DIL_FAR))
    assert s % (BLOCK * DIL_FAR) == 0 and s // DIL_FAR >= 2 * BLOCK
    gb = GROUP_BLOCKS
    assert s // DIL_MID // BLOCK == gb and gb % (s // DIL_FAR // BLOCK) == 0 and (s // BLOCK) % gb == 0
    tok4 = lambda t: t.reshape(b, 1, s, w)
    spec = lambda dil: pl.BlockSpec((None, dil, s // dil, LANES), lambda bi, g: (bi, 0, 0, g))
    vshape = lambda dil: pltpu.VMEM((dil, s // dil, LANES), BF16)
    return pl.pallas_call(
        functools.partial(_attn_kernel, span=BLOCK),
        grid=(b, w // LANES),
        in_specs=[spec(1)] * 3 + [spec(DIL_MID)] * 3 + [spec(DIL_FAR)] * 3,
        out_specs=pl.BlockSpec((None, s, LANES), lambda bi, g: (bi, 0, g)),
        out_shape=jax.ShapeDtypeStruct((b, s, w), BF16),
        scratch_shapes=[vshape(1), vshape(1), vshape(DIL_MID), vshape(DIL_MID), vshape(DIL_FAR),
                        vshape(DIL_FAR)]
                       + [pltpu.VMEM((s, LANES), F32)] * 3 + [pltpu.VMEM((BLOCK, LANES), F32)] * 3
                       + [pltpu.VMEM((2, BLOCK, 2 * BLOCK), F32),
                          pltpu.VMEM((2 * gb, BLOCK, 2 * BLOCK), F32),
                          pltpu.VMEM((2 * gb, BLOCK, 2 * BLOCK), BF16),
                          pltpu.VMEM((2 * gb, BLOCK, LANES), F32)],
        compiler_params=_cparams(("parallel", "parallel")),
        name="attn",
    )(tok4(q1), tok4(k1), tok4(v1), q4, k4, v4, q16, k16, v16)


def _rec_kernel(xr_ref, gr_ref, cw_ref, cb_ref, wr_ref, br_ref, wi_ref, bi_ref, lam_ref, g_ref,
                out_ref, xe, a_s, h_s, carry, *, pitch):
    ts = xr_ref.shape[0]
    pad = SUBLANES

    @pl.when(pl.program_id(1) == 0)
    def _():
        xe[0:pad, :] = jnp.zeros((pad, xe.shape[1]), F32)
        carry[...] = jnp.zeros_like(carry)

    x = xr_ref[...]
    xe[pad:, :] = x
    xc = cb_ref[...] + cw_ref[REC_CONV - 1:REC_CONV, :] * x
    for kk in range(1, REC_CONV):
        xc = xc + cw_ref[REC_CONV - 1 - kk:REC_CONV - kk, :] * xe[pl.ds(pad - kk, ts), :]
    xe[0:pad, :] = x[ts - pad:, :]

    xb = xc.astype(BF16)
    r = jax.nn.sigmoid(jnp.dot(xb, wr_ref[...], preferred_element_type=F32) + br_ref[...])
    i = jax.nn.sigmoid(jnp.dot(xb, wi_ref[...], preferred_element_type=F32) + bi_ref[...])
    z = -lam_ref[...]
    softplus = jnp.maximum(z, 0.0) + jnp.log1p(jnp.exp(-jnp.abs(z)))
    log_a = (-LRU_C * softplus) * r
    a = jnp.exp(log_a)
    w = jnp.tanh(-log_a) * (a * a + 1.0)
    u = jnp.where(w > 0.0, w * lax.rsqrt(w), 0.0) * (i * xc)

    nch = SUBLANES
    clen = ts // nch
    nslab = a_s.shape[0]
    for g in range(nslab):
        cols = slice(g * LANES, (g + 1) * LANES)
        for j in range(nch):
            a_s[g, j * pitch:j * pitch + clen, :] = a[j * clen:(j + 1) * clen, cols]
            h_s[g, j * pitch:j * pitch + clen, :] = u[j * clen:(j + 1) * clen, cols]

    def step(s, hp):
        hs, ps = hp
        rows = pl.ds(s, nch, stride=pitch)
        new_h, new_p = [], []
        for g in range(nslab):
            av = a_s[g, rows, :]
            hv = av * hs[g] + h_s[g, rows, :]
            pv = av * ps[g]
            h_s[g, rows, :] = hv
            a_s[g, rows, :] = pv
            new_h.append(hv)
            new_p.append(pv)
        return tuple(new_h), tuple(new_p)

    zero = jnp.zeros((nch, LANES), F32)
    h_end, p_end = lax.fori_loop(0, clen, step, ((zero,) * nslab, (zero + 1.0,) * nslab))

    carry_in = []
    for g in range(nslab):
        cols = slice(g * LANES, (g + 1) * LANES)
        cj = carry[:, cols]
        per_chunk = []
        for j in range(nch):
            per_chunk.append(cj)
            cj = h_end[g][j:j + 1, :] + p_end[g][j:j + 1, :] * cj
        carry[:, cols] = cj
        carry_in.append(per_chunk)

    for j in range(nch):
        rows = slice(j * pitch, j * pitch + clen)
        h = jnp.concatenate([h_s[g, rows, :] + a_s[g, rows, :] * carry_in[g][j] for g in range(nslab)], axis=1)
        y = _gelu_times(gr_ref[j * clen:(j + 1) * clen, :], h)
        ms = jnp.mean(y * y, axis=-1, keepdims=True)
        out_ref[j * clen:(j + 1) * clen, :] = (y * lax.rsqrt(ms + EPS) * g_ref[...]).astype(out_ref.dtype)


def _rec(xr, gr, cw, cb, wr_bd, br, wi_bd, bi, lam, g, *, ts=512):
    b, s, w = xr.shape
    blk = pl.BlockSpec((None, ts, w), lambda bi_, t: (bi_, t, 0))
    vec = _const_spec((1, w))
    pitch = ts // SUBLANES + 4
    assert (ts // SUBLANES) % 8 == 0
    scan_buf = pltpu.VMEM((w // LANES, SUBLANES * pitch, LANES), F32)
    return pl.pallas_call(
        functools.partial(_rec_kernel, pitch=pitch),
        grid=(b, s // ts),
        in_specs=[blk, blk, _const_spec(cw.shape), vec, _const_spec(wr_bd.shape), vec,
                  _const_spec(wi_bd.shape), vec, vec, vec],
        out_specs=blk,
        out_shape=jax.ShapeDtypeStruct((b, s, w), BF16),
        scratch_shapes=[pltpu.VMEM((ts + SUBLANES, w), F32), scan_buf, scan_buf,
                        pltpu.VMEM((1, w), F32)],
        compiler_params=_cparams(("parallel", "arbitrary")),
        name="rec",
    )(xr, gr, cw, cb, wr_bd, br, wi_bd, bi, lam, g)


def _ffn_kernel(x_ref, attn_ref, rec_ref, ga_ref, wo_ref, g_ref, wup_ref, cw_ref, cb_ref, wdn_ref,
                out_ref, ua, ub, tail, acc, h_s, act_a, act_b, *, cf):
    tm = x_ref.shape[0]
    pad = SUBLANES
    d_ff = wdn_ref.shape[0]
    nchunks = d_ff // cf
    ubufs = (ua, ub)
    acts = (act_a, act_b)

    @pl.when(pl.program_id(1) == 0)
    def _():
        tail[...] = jnp.zeros_like(tail)
        for buf in ubufs:
            buf[:, pad + tm:, :] = jnp.zeros((FFN_CONV, pad, buf.shape[2]), F32)

    attn = attn_ref[...].astype(F32)
    ms = jnp.mean(attn * attn, axis=-1, keepdims=True)
    aw = attn.shape[1]
    h_s[:, 0:aw] = (attn * lax.rsqrt(ms + EPS) * ga_ref[...]).astype(BF16)
    h_s[:, aw:] = rec_ref[...]
    x = x_ref[...] + jnp.dot(h_s[...], wo_ref[...], preferred_element_type=F32)
    ms = jnp.mean(x * x, axis=-1, keepdims=True)
    h_s[...] = (x * lax.rsqrt(ms + EPS) * g_ref[...]).astype(BF16)
    acc[...] = x

    def halves(j):
        for half in range(2):
            yield (slice(half * d_ff + j * cf, half * d_ff + (j + 1) * cf),
                   slice(half * cf, (half + 1) * cf))

    def up(j):
        buf = ubufs[j % 2]
        for cols, dst in halves(j):
            u = jnp.dot(h_s[...], wup_ref[:, cols], preferred_element_type=F32)
            buf[0, pad:pad + tm, dst] = u
            for kk in range(1, FFN_CONV):
                buf[kk, pad:2 * pad, dst] = tail[kk - 1, :, cols]
                buf[kk, pad + kk:pad + kk + tm, dst] = u
                tail[kk - 1, :, cols] = buf[kk, pad + tm:2 * pad + tm, dst]

    def elementwise(j):
        buf = ubufs[j % 2]
        parts = []
        for cols, dst in halves(j):
            uc = cb_ref[:, cols]
            for kk in range(FFN_CONV):
                uc = uc + cw_ref[FFN_CONV - 1 - kk:FFN_CONV - kk, cols] * buf[kk, pad:pad + tm, dst]
            parts.append(uc)
        acts[j % 2][...] = _gelu_times(parts[0], parts[1]).astype(BF16)

    def down(j):
        acc[...] += jnp.dot(acts[j % 2][...], wdn_ref[j * cf:(j + 1) * cf, :], preferred_element_type=F32)

    up(0)
    for j in range(nchunks + 1):
        if j + 1 < nchunks:
            up(j + 1)
        if j < nchunks:
            elementwise(j)
        if j >= 1:
            down(j - 1)
    out_ref[...] = acc[...]


def _ffn(x, attn, rec, ga, wo, g, wup_b, cw, cb, wdn_b, *, tm=512, cf=512):
    b, s, d = x.shape
    d_ff = wdn_b.shape[0]
    assert d_ff % cf == 0
    row = lambda w: pl.BlockSpec((None, tm, w), lambda bi, t: (bi, t, 0))
    blk = row(d)
    return pl.pallas_call(
        functools.partial(_ffn_kernel, cf=cf),
        grid=(b, s // tm),
        in_specs=[blk, row(attn.shape[-1]), row(rec.shape[-1]), _const_spec(ga.shape),
                  _const_spec(wo.shape),
                  _const_spec((1, d)), _const_spec(wup_b.shape), _const_spec(cw.shape),
                  _const_spec(cb.shape), _const_spec(wdn_b.shape)],
        out_specs=blk,
        out_shape=jax.ShapeDtypeStruct((b, s, d), F32),
        scratch_shapes=[pltpu.VMEM((FFN_CONV, tm + 2 * SUBLANES, 2 * cf), F32)] * 2
                       + [pltpu.VMEM((FFN_CONV - 1, SUBLANES, 2 * d_ff), F32), pltpu.VMEM((tm, d), F32),
                          pltpu.VMEM((tm, d), BF16), pltpu.VMEM((tm, cf), BF16),
                          pltpu.VMEM((tm, cf), BF16)],
        compiler_params=_cparams(("parallel", "arbitrary")),
        name="ffn",
    )(x, attn, rec, ga, wo, g, wup_b, cw, cb, wdn_b)


def _block_diag(w):
    n, c, _ = w.shape
    eye = jnp.eye(n, dtype=w.dtype)
    return (eye[:, None, :, None] * w[:, :, None, :]).reshape(n * c, n * c)


def _pair_rotary_layout(a):
    lead = a.shape[:-1]
    half = HEAD_DIM // 2
    a = a.reshape(*lead, -1, 2, 2, half)
    return jnp.swapaxes(a, -3, -2).reshape(*lead, -1)


def kernel(x, positions, g_mix, w_in, q_norm_g, k_norm_g, rec_conv_w, rec_conv_b, w_rg, b_rg, w_ig,
           b_ig, lru_lambda, g_attn_out, g_rec_out, w_out, g_ffn, w_up, ffn_conv_w, ffn_conv_b, w_down):
    bsz, s, d = x.shape
    t = bsz * s
    depth = w_in.shape[0]
    rw = rec_conv_w.shape[-1]
    aw = w_out.shape[1] - rw
    n_heads = aw // HEAD_DIM

    half = HEAD_DIM // 2
    inv_freq = ROPE_THETA ** (-jnp.arange(half, dtype=F32) / half)
    invf = jnp.tile(inv_freq, LANES // half).reshape(1, LANES)
    pos = positions.astype(F32).reshape(bsz, s, 1)

    for layer in range(depth):
        w_l = w_in[layer].astype(BF16)
        w_l = jnp.concatenate([_pair_rotary_layout(w_l[:, :aw]), _pair_rotary_layout(w_l[:, aw:2 * aw]),
                               w_l[:, 2 * aw:]], axis=1)
        *qkv, xr, gr = _inproj(
            x, pos, g_mix[layer].reshape(1, d), w_l,
            _pair_rotary_layout(jnp.tile(q_norm_g[layer], n_heads)).reshape(1, aw),
            _pair_rotary_layout(jnp.tile(k_norm_g[layer], n_heads)).reshape(1, aw), invf, aw=aw, rw=rw)

        attn = _attention(*qkv)

        rec = _rec(xr, gr, rec_conv_w[layer],
                   rec_conv_b[layer].reshape(1, rw), _block_diag(w_rg[layer]).astype(BF16),
                   b_rg[layer].reshape(1, rw), _block_diag(w_ig[layer]).astype(BF16),
                   b_ig[layer].reshape(1, rw), lru_lambda[layer].reshape(1, rw),
                   g_rec_out[layer].reshape(1, rw))

        w_out_b = w_out[layer].astype(BF16)
        x = _ffn(x, attn, rec, g_attn_out[layer].reshape(1, aw), w_out_b,
                 g_ffn[layer].reshape(1, d), w_up[layer].astype(BF16),
                 ffn_conv_w[layer], ffn_conv_b[layer].reshape(1, -1), w_down[layer].astype(BF16))
    return x
```

```python
import functools

import jax
import jax.numpy as jnp
import numpy as np
from jax import lax
from jax.experimental import pallas as pl
from jax.experimental.pallas import tpu as pltpu

HEAD_DIM = 64
REC_CONV = 4
LRU_C = 8.0
FFN_CONV = 3
WINDOW_DILATIONS = ((128, 1), (512, 4), (2048, 16))
DIL_MID, DIL_FAR = 4, 16
BLOCK = 128
ROPE_THETA = 10000.0
EPS = 1e-6
NEG_INF = -1e30

LANES = 128
SUBLANES = 8
VMEM_LIMIT = 56 * 1024 * 1024

F32 = jnp.float32
BF16 = jnp.bfloat16


def _cparams(sem, flags=None):
    return pltpu.CompilerParams(dimension_semantics=sem, vmem_limit_bytes=VMEM_LIMIT, flags=flags)


_GELU_C = float(np.float32(np.sqrt(2.0 / np.pi)))
_GELU_CK = _GELU_C * float(np.float32(0.044715))


def _gelu_times(g, other):
    t = jnp.tanh(g * (_GELU_C + _GELU_CK * (g * g)))
    half = (0.5 * g) * other
    return half + half * t


def _const_spec(shape):
    nd = len(shape)
    return pl.BlockSpec(shape, lambda *_: (0,) * nd)


def _inproj_kernel(x_ref, pos_ref, gmix_ref, w_ref, qg_ref, kg_ref, invf_ref,
                   q1, k1, v1, q4, k4, v4, q16, k16, v16, xr_ref, gr_ref, s1, s4, *, aw, rw, nsub):
    tm = x_ref.shape[0]
    rows_per = tm // nsub
    nslab = aw // LANES
    lane = lax.broadcasted_iota(jnp.int32, (rows_per, LANES), 1)
    head_a = (lane & (HEAD_DIM // 2)) == 0
    n4 = rows_per // DIL_MID
    n16 = rows_per // DIL_FAR

    for sub in range(nsub):
        rows = slice(sub * rows_per, (sub + 1) * rows_per)
        x = x_ref[rows, :]
        ms = jnp.mean(x * x, axis=-1, keepdims=True)
        h = (x * lax.rsqrt(ms + EPS) * gmix_ref[...]).astype(BF16)

        ang = pos_ref[rows, :] * invf_ref[...]
        cos = jnp.cos(ang)
        sin = jnp.sin(ang)
        sin_signed = jnp.where(lane < HEAD_DIM, -sin, sin)

        def head_norm_rotary(p, g_ref):
            outs = []
            for g in range(nslab):
                cols = slice(g * LANES, (g + 1) * LANES)
                pg = p[:, cols]
                sq = pg * pg
                s_a = jnp.sum(jnp.where(head_a, sq, 0.0), axis=-1, keepdims=True)
                s_b = jnp.sum(jnp.where(head_a, 0.0, sq), axis=-1, keepdims=True)
                ms_h = jnp.where(head_a, s_a, s_b) * (1.0 / HEAD_DIM)
                xg = pg * lax.rsqrt(ms_h + EPS) * g_ref[:, cols]
                outs.append(xg * cos + pltpu.roll(xg, HEAD_DIM, 1) * sin_signed)
            return jnp.concatenate(outs, axis=1)

        def emit(which, val, o1, o4, o16):
            o1[rows, :] = val.astype(BF16)
            base = sub * rows_per
            for g in range(nslab):
                cols = slice(g * LANES, (g + 1) * LANES)
                s1[which, g, rows, :] = val[:, cols]
                for c in range(DIL_MID):
                    t4 = s1[which, g, pl.ds(base + c, n4, stride=DIL_MID), :]
                    o4[c, sub * n4:(sub + 1) * n4, cols] = t4.astype(BF16)
                    s4[which, g, base + c * n4:base + (c + 1) * n4, :] = t4
                for c4 in range(DIL_MID):
                    for cp in range(DIL_FAR // DIL_MID):
                        t16 = s4[which, g, pl.ds(base + c4 * n4 + cp, n16, stride=DIL_MID), :]
                        o16[c4 + DIL_MID * cp, sub * n16:(sub + 1) * n16, cols] = t16.astype(BF16)

        qp = jnp.dot(h, w_ref[:, 0:aw], preferred_element_type=F32)
        kp = jnp.dot(h, w_ref[:, aw:2 * aw], preferred_element_type=F32)
        vp = jnp.dot(h, w_ref[:, 2 * aw:3 * aw], preferred_element_type=F32)
        emit(0, head_norm_rotary(qp, qg_ref) * (HEAD_DIM ** -0.5), q1, q4, q16)
        xr_ref[rows, :] = jnp.dot(h, w_ref[:, 3 * aw:3 * aw + rw], preferred_element_type=F32)
        emit(1, head_norm_rotary(kp, kg_ref), k1, k4, k16)
        gr_ref[rows, :] = jnp.dot(h, w_ref[:, 3 * aw + rw:3 * aw + 2 * rw], preferred_element_type=F32)
        emit(2, vp, v1, v4, v16)


def _inproj(x, pos, g_mix, w_in_b, qg, kg, invf, *, aw, rw, tm=512):
    b, s, d = x.shape
    row = lambda w: pl.BlockSpec((None, tm, w), lambda bi, i: (bi, i, 0))
    grp = lambda dil: pl.BlockSpec((None, dil, tm // dil, aw), lambda bi, i: (bi, 0, i, 0))
    grp_shape = lambda dil: jax.ShapeDtypeStruct((b, dil, s // dil, aw), BF16)
    tok_shape = jax.ShapeDtypeStruct((b, s, aw), BF16)
    return pl.pallas_call(
        functools.partial(_inproj_kernel, aw=aw, rw=rw, nsub=2),
        grid=(b, s // tm),
        in_specs=[row(d), row(1), _const_spec((1, d)), _const_spec(w_in_b.shape),
                  _const_spec((1, aw)), _const_spec((1, aw)),
                  _const_spec((1, LANES))],
        out_specs=[row(aw)] * 3 + [grp(DIL_MID)] * 3 + [grp(DIL_FAR)] * 3 + [row(rw)] * 2,
        out_shape=[tok_shape] * 3 + [grp_shape(DIL_MID)] * 3 + [grp_shape(DIL_FAR)] * 3
                  + [jax.ShapeDtypeStruct((b, s, rw), F32)] * 2,
        scratch_shapes=[pltpu.VMEM((3, aw // LANES, tm, LANES), F32)] * 2,
        compiler_params=_cparams(("parallel", "parallel")),
        name="inproj",
    )(x, pos, g_mix, w_in_b, qg, kg, invf)


GROUP_BLOCKS = 8


def _attn_kernel(q1, k1, v1, q4, k4, v4, q16, k16, v16, out_ref,
                 va1, vb1, va4, vb4, va16, vb16, acc_s, m_s, l_s, tmp_a, tmp_m, tmp_l,
                 bias_s, s_a, s_b, m_a, m_b, *, span):
    seq = out_ref.shape[0]
    grp = seq // DIL_MID
    gb = GROUP_BLOCKS
    lane = lax.broadcasted_iota(jnp.int32, (BLOCK, LANES), 1)
    low_head = lane < HEAD_DIM
    q_head_a = (lane & (HEAD_DIM // 2)) == 0
    q_head_b = jnp.logical_not(q_head_a)

    qi = lax.broadcasted_iota(jnp.int32, (BLOCK, 2 * BLOCK), 0)
    kj = lax.broadcasted_iota(jnp.int32, (BLOCK, 2 * BLOCK), 1)
    for e in range(2):
        rel = qi - kj + e * BLOCK
        bias_s[e] = jnp.where((rel >= 0) & (rel <= span), 0.0, NEG_INF).astype(F32)

    for v_ref, va, vb in ((v1, va1, vb1), (v4, va4, vb4), (v16, va16, vb16)):
        for c in range(v_ref.shape[0]):
            vv = v_ref[c]
            lo = lax.broadcasted_iota(jnp.int32, vv.shape, 1) < HEAD_DIM
            one = jnp.ones_like(vv)
            va[c] = jnp.where(lo, vv, one)
            vb[c] = jnp.where(lo, one, vv)

    def rows_of(i):
        return i * BLOCK, max(i - 1, 0) * BLOCK, min(i, 1)

    def scores(group, bufs):
        q_ref, k_ref, _, _, coords, _ = group
        s_buf, m_buf = bufs
        for t, (c, i) in enumerate(coords):
            r0, start, e = rows_of(i)
            qp = q_ref[c, r0:r0 + BLOCK, :]
            ks = k_ref[c, start:start + 2 * BLOCK, :]
            for h, sel in enumerate((q_head_a, q_head_b)):
                qm = jnp.where(sel, qp, jnp.zeros_like(qp))
                s = lax.dot_general(qm, ks, (((1,), (1,)), ((), ())), preferred_element_type=F32)
                s = s + bias_s[e]
                s_buf[2 * t + h] = s
                m = jnp.max(jnp.maximum(s[:, 0:BLOCK], s[:, BLOCK:]), axis=-1, keepdims=True)
                m_buf[2 * t + h] = jnp.broadcast_to(m, (BLOCK, LANES))

    def finish(group, bufs):
        _, _, va, vb, coords, consume = group
        s_buf, m_buf = bufs
        for t, (c, i) in enumerate(coords):
            _, start, _ = rows_of(i)
            pvs = []
            for h, v_ref in enumerate((va, vb)):
                hb = 2 * t + h
                m = m_buf[hb]
                p = jnp.concatenate([jnp.exp(s_buf[hb, :, 0:BLOCK] - m).astype(BF16),
                                     jnp.exp(s_buf[hb, :, BLOCK:] - m).astype(BF16)], axis=1)
                pvs.append(jnp.dot(p, v_ref[c, start:start + 2 * BLOCK, :], preferred_element_type=F32))
            acc = jnp.where(low_head, pvs[0], pvs[1])
            l = pltpu.roll(jnp.where(low_head, pvs[1], pvs[0]), HEAD_DIM, 1)
            m = jnp.where(low_head, m_buf[2 * t], m_buf[2 * t + 1])
            consume(c, i, (acc, m, l))

    def merge(old, new):
        acc0, m0, l0 = old
        acc1, m1, l1 = new
        mn = jnp.maximum(m0, m1)
        e0 = jnp.exp(m0 - mn)
        e1 = jnp.exp(m1 - mn)
        return e0 * acc0 + e1 * acc1, mn, e0 * l0 + e1 * l1

    def mid_store(c, i, new):
        r0 = c * grp + i * BLOCK
        for ref, val in zip((acc_s, m_s, l_s), new):
            ref[r0:r0 + BLOCK, :] = val

    ratio = DIL_FAR // DIL_MID
    far_blocks = seq // DIL_FAR // BLOCK

    def far_merge(c, i, new):
        row0 = (c % DIL_MID) * grp + c // DIL_MID + i * (BLOCK * ratio)
        rows = pl.ds(row0, BLOCK, stride=ratio)
        merged = merge((acc_s[rows, :], m_s[rows, :], l_s[rows, :]), new)
        for ref, val in zip((acc_s, m_s, l_s), merged):
            ref[rows, :] = val

    def near_emit(c, i, new):
        per = BLOCK // DIL_MID
        for r in range(DIL_MID):
            src = slice(r * grp + i * per, r * grp + (i + 1) * per)
            dst = pl.ds(r, per, stride=DIL_MID)
            tmp_a[dst, :] = acc_s[src, :]
            tmp_m[dst, :] = m_s[src, :]
            tmp_l[dst, :] = l_s[src, :]
        acc, _, l = merge((tmp_a[...], tmp_m[...], tmp_l[...]), new)
        out_ref[i * BLOCK:(i + 1) * BLOCK, :] = (acc / l).astype(out_ref.dtype)

    groups = [(q4, k4, va4, vb4, [(c, t) for t in range(gb)], mid_store) for c in range(DIL_MID)]
    groups += [(q16, k16, va16, vb16,
                [(gi * (gb // far_blocks) + t // far_blocks, t % far_blocks) for t in range(gb)], far_merge)
               for gi in range(DIL_FAR * far_blocks // gb)]
    groups += [(q1, k1, va1, vb1, [(0, gi * gb + t) for t in range(gb)], near_emit)
               for gi in range(seq // BLOCK // gb)]

    s_bufs = ((s_a, m_a), (s_b, m_b))
    scores(groups[0], s_bufs[0])
    for k, group in enumerate(groups):
        if k + 1 < len(groups):
            scores(groups[k + 1], s_bufs[(k + 1) % 2])
        finish(group, s_bufs[k % 2])


def _attention(q1, k1, v1, q4, k4, v4, q16, k16, v16):
    b, s, w = q1.shape
    assert WINDOW_DILATIONS == ((BLOCK, 1), (BLOCK * DIL_MID, DIL_MID), (BLOCK * DIL_FAR, DIL_FAR))
    assert s % (BLOCK * DIL_FAR) == 0 and s // DIL_FAR >= 2 * BLOCK
    gb = GROUP_BLOCKS
    assert s // DIL_MID // BLOCK == gb and gb % (s // DIL_FAR // BLOCK) == 0 and (s // BLOCK) % gb == 0
    tok4 = lambda t: t.reshape(b, 1, s, w)
    spec = lambda dil: pl.BlockSpec((None, dil, s // dil, LANES), lambda bi, g: (bi, 0, 0, g))
    vshape = lambda dil: pltpu.VMEM((dil, s // dil, LANES), BF16)
    return pl.pallas_call(
        functools.partial(_attn_kernel, span=BLOCK),
        grid=(b, w // LANES),
        in_specs=[spec(1)] * 3 + [spec(DIL_MID)] * 3 + [spec(DIL_FAR)] * 3,
        out_specs=pl.BlockSpec((None, s, LANES), lambda bi, g: (bi, 0, g)),
        out_shape=jax.ShapeDtypeStruct((b, s, w), BF16),
        scratch_shapes=[vshape(1), vshape(1), vshape(DIL_MID), vshape(DIL_MID), vshape(DIL_FAR),
                        vshape(DIL_FAR)]
                       + [pltpu.VMEM((s, LANES), F32)] * 3 + [pltpu.VMEM((BLOCK, LANES), F32)] * 3
                       + [pltpu.VMEM((2, BLOCK, 2 * BLOCK), F32),
                          pltpu.VMEM((2 * gb, BLOCK, 2 * BLOCK), F32),
                          pltpu.VMEM((2 * gb, BLOCK, 2 * BLOCK), F32),
                          pltpu.VMEM((2 * gb, BLOCK, LANES), F32),
                          pltpu.VMEM((2 * gb, BLOCK, LANES), F32)],
        compiler_params=_cparams(("parallel", "parallel")),
        name="attn",
    )(tok4(q1), tok4(k1), tok4(v1), q4, k4, v4, q16, k16, v16)


def _rec_kernel(xr_ref, gr_ref, cw_ref, cb_ref, wr_ref, br_ref, wi_ref, bi_ref, lam_ref, g_ref,
                out_ref, xe, a_s, h_s, carry, *, pitch):
    ts = xr_ref.shape[0]
    pad = SUBLANES

    @pl.when(pl.program_id(1) == 0)
    def _():
        xe[0:pad, :] = jnp.zeros((pad, xe.shape[1]), F32)
        carry[...] = jnp.zeros_like(carry)

    x = xr_ref[...]
    xe[pad:, :] = x
    xc = cb_ref[...] + cw_ref[REC_CONV - 1:REC_CONV, :] * x
    for kk in range(1, REC_CONV):
        xc = xc + cw_ref[REC_CONV - 1 - kk:REC_CONV - kk, :] * xe[pl.ds(pad - kk, ts), :]
    xe[0:pad, :] = x[ts - pad:, :]

    xb = xc.astype(BF16)
    r = jax.nn.sigmoid(jnp.dot(xb, wr_ref[...], preferred_element_type=F32) + br_ref[...])
    i = jax.nn.sigmoid(jnp.dot(xb, wi_ref[...], preferred_element_type=F32) + bi_ref[...])
    z = -lam_ref[...]
    softplus = jnp.maximum(z, 0.0) + jnp.log1p(jnp.exp(-jnp.abs(z)))
    log_a = (-LRU_C * softplus) * r
    a = jnp.exp(log_a)
    w = jnp.tanh(-log_a) * (a * a + 1.0)
    u = jnp.where(w > 0.0, w * lax.rsqrt(w), 0.0) * (i * xc)

    nch = SUBLANES
    clen = ts // nch
    nslab = a_s.shape[0]
    for g in range(nslab):
        cols = slice(g * LANES, (g + 1) * LANES)
        for j in range(nch):
            a_s[g, j * pitch:j * pitch + clen, :] = a[j * clen:(j + 1) * clen, cols]
            h_s[g, j * pitch:j * pitch + clen, :] = u[j * clen:(j + 1) * clen, cols]

    def step(s, hp):
        hs, ps = hp
        rows = pl.ds(s, nch, stride=pitch)
        new_h, new_p = [], []
        for g in range(nslab):
            av = a_s[g, rows, :]
            hv = av * hs[g] + h_s[g, rows, :]
            pv = av * ps[g]
            h_s[g, rows, :] = hv
            a_s[g, rows, :] = pv
            new_h.append(hv)
            new_p.append(pv)
        return tuple(new_h), tuple(new_p)

    zero = jnp.zeros((nch, LANES), F32)
    h_end, p_end = lax.fori_loop(0, clen, step, ((zero,) * nslab, (zero + 1.0,) * nslab))

    carry_in = []
    for g in range(nslab):
        cols = slice(g * LANES, (g + 1) * LANES)
        cj = carry[:, cols]
        per_chunk = []
        for j in range(nch):
            per_chunk.append(cj)
            cj = h_end[g][j:j + 1, :] + p_end[g][j:j + 1, :] * cj
        carry[:, cols] = cj
        carry_in.append(per_chunk)

    for j in range(nch):
        rows = slice(j * pitch, j * pitch + clen)
        h = jnp.concatenate([h_s[g, rows, :] + a_s[g, rows, :] * carry_in[g][j] for g in range(nslab)], axis=1)
        y = _gelu_times(gr_ref[j * clen:(j + 1) * clen, :], h)
        ms = jnp.mean(y * y, axis=-1, keepdims=True)
        out_ref[j * clen:(j + 1) * clen, :] = (y * lax.rsqrt(ms + EPS) * g_ref[...]).astype(out_ref.dtype)


def _rec(xr, gr, cw, cb, wr_bd, br, wi_bd, bi, lam, g, *, ts=512):
    b, s, w = xr.shape
    blk = pl.BlockSpec((None, ts, w), lambda bi_, t: (bi_, t, 0))
    vec = _const_spec((1, w))
    pitch = ts // SUBLANES + 4
    assert (ts // SUBLANES) % 8 == 0
    scan_buf = pltpu.VMEM((w // LANES, SUBLANES * pitch, LANES), F32)
    return pl.pallas_call(
        functools.partial(_rec_kernel, pitch=pitch),
        grid=(b, s // ts),
        in_specs=[blk, blk, _const_spec(cw.shape), vec, _const_spec(wr_bd.shape), vec,
                  _const_spec(wi_bd.shape), vec, vec, vec],
        out_specs=blk,
        out_shape=jax.ShapeDtypeStruct((b, s, w), BF16),
        scratch_shapes=[pltpu.VMEM((ts + SUBLANES, w), F32), scan_buf, scan_buf,
                        pltpu.VMEM((1, w), F32)],
        compiler_params=_cparams(("parallel", "arbitrary")),
        name="rec",
    )(xr, gr, cw, cb, wr_bd, br, wi_bd, bi, lam, g)


def _ffn_kernel(x_ref, attn_ref, rec_ref, ga_ref, wo_ref, g_ref, wup_ref, cw_ref, cb_ref, wdn_ref,
                out_ref, ua, ub, tail, acc, h_s, act_a, act_b, *, cf):
    tm = x_ref.shape[0]
    pad = SUBLANES
    d_ff = wdn_ref.shape[0]
    nchunks = d_ff // cf
    ubufs = (ua, ub)
    acts = (act_a, act_b)

    @pl.when(pl.program_id(1) == 0)
    def _():
        tail[...] = jnp.zeros_like(tail)
        for buf in ubufs:
            buf[:, pad + tm:, :] = jnp.zeros((FFN_CONV, pad, buf.shape[2]), F32)

    attn = attn_ref[...].astype(F32)
    ms = jnp.mean(attn * attn, axis=-1, keepdims=True)
    aw = attn.shape[1]
    h_s[:, 0:aw] = (attn * lax.rsqrt(ms + EPS) * ga_ref[...]).astype(BF16)
    h_s[:, aw:] = rec_ref[...]
    x = x_ref[...] + jnp.dot(h_s[...], wo_ref[...], preferred_element_type=F32)
    ms = jnp.mean(x * x, axis=-1, keepdims=True)
    h_s[...] = (x * lax.rsqrt(ms + EPS) * g_ref[...]).astype(BF16)
    acc[...] = x

    def halves(j):
        for half in range(2):
            yield (slice(half * d_ff + j * cf, half * d_ff + (j + 1) * cf),
                   slice(half * cf, (half + 1) * cf))

    def up(j):
        buf = ubufs[j % 2]
        for cols, dst in halves(j):
            u = jnp.dot(h_s[...], wup_ref[:, cols], preferred_element_type=F32)
            buf[0, pad:pad + tm, dst] = u
            for kk in range(1, FFN_CONV):
                buf[kk, pad:2 * pad, dst] = tail[kk - 1, :, cols]
                buf[kk, pad + kk:pad + kk + tm, dst] = u
                tail[kk - 1, :, cols] = buf[kk, pad + tm:2 * pad + tm, dst]

    def elementwise(j):
        buf = ubufs[j % 2]
        parts = []
        for cols, dst in halves(j):
            uc = cb_ref[:, cols]
            for kk in range(FFN_CONV):
                uc = uc + cw_ref[FFN_CONV - 1 - kk:FFN_CONV - kk, cols] * buf[kk, pad:pad + tm, dst]
            parts.append(uc)
        acts[j % 2][...] = _gelu_times(parts[0], parts[1]).astype(BF16)

    def down(j):
        acc[...] += jnp.dot(acts[j % 2][...], wdn_ref[j * cf:(j + 1) * cf, :], preferred_element_type=F32)

    up(0)
    for j in range(nchunks + 1):
        if j + 1 < nchunks:
            up(j + 1)
        if j < nchunks:
            elementwise(j)
        if j >= 1:
            down(j - 1)
    out_ref[...] = acc[...]


def _ffn(x, attn, rec, ga, wo, g, wup_b, cw, cb, wdn_b, *, tm=512, cf=512):
    b, s, d = x.shape
    d_ff = wdn_b.shape[0]
    assert d_ff % cf == 0
    row = lambda w: pl.BlockSpec((None, tm, w), lambda bi, t: (bi, t, 0))
    blk = row(d)
    return pl.pallas_call(
        functools.partial(_ffn_kernel, cf=cf),
        grid=(b, s // tm),
        in_specs=[blk, row(attn.shape[-1]), row(rec.shape[-1]), _const_spec(ga.shape),
                  _const_spec(wo.shape),
                  _const_spec((1, d)), _const_spec(wup_b.shape), _const_spec(cw.shape),
                  _const_spec(cb.shape), _const_spec(wdn_b.shape)],
        out_specs=blk,
        out_shape=jax.ShapeDtypeStruct((b, s, d), F32),
        scratch_shapes=[pltpu.VMEM((FFN_CONV, tm + 2 * SUBLANES, 2 * cf), F32)] * 2
                       + [pltpu.VMEM((FFN_CONV - 1, SUBLANES, 2 * d_ff), F32), pltpu.VMEM((tm, d), F32),
                          pltpu.VMEM((tm, d), BF16), pltpu.VMEM((tm, cf), BF16),
                          pltpu.VMEM((tm, cf), BF16)],
        compiler_params=_cparams(("parallel", "arbitrary")),
        name="ffn",
    )(x, attn, rec, ga, wo, g, wup_b, cw, cb, wdn_b)


def _block_diag(w):
    n, c, _ = w.shape
    eye = jnp.eye(n, dtype=w.dtype)
    return (eye[:, None, :, None] * w[:, :, None, :]).reshape(n * c, n * c)


def _pair_rotary_layout(a):
    lead = a.shape[:-1]
    half = HEAD_DIM // 2
    a = a.reshape(*lead, -1, 2, 2, half)
    return jnp.swapaxes(a, -3, -2).reshape(*lead, -1)


def kernel(x, positions, g_mix, w_in, q_norm_g, k_norm_g, rec_conv_w, rec_conv_b, w_rg, b_rg, w_ig,
           b_ig, lru_lambda, g_attn_out, g_rec_out, w_out, g_ffn, w_up, ffn_conv_w, ffn_conv_b, w_down):
    bsz, s, d = x.shape
    t = bsz * s
    depth = w_in.shape[0]
    rw = rec_conv_w.shape[-1]
    aw = w_out.shape[1] - rw
    n_heads = aw // HEAD_DIM

    half = HEAD_DIM // 2
    inv_freq = ROPE_THETA ** (-jnp.arange(half, dtype=F32) / half)
    invf = jnp.tile(inv_freq, LANES // half).reshape(1, LANES)
    pos = positions.astype(F32).reshape(bsz, s, 1)

    for layer in range(depth):
        w_l = w_in[layer].astype(BF16)
        w_l = jnp.concatenate([_pair_rotary_layout(w_l[:, :aw]), _pair_rotary_layout(w_l[:, aw:2 * aw]),
                               w_l[:, 2 * aw:]], axis=1)
        *qkv, xr, gr = _inproj(
            x, pos, g_mix[layer].reshape(1, d), w_l,
            _pair_rotary_layout(jnp.tile(q_norm_g[layer], n_heads)).reshape(1, aw),
            _pair_rotary_layout(jnp.tile(k_norm_g[layer], n_heads)).reshape(1, aw), invf, aw=aw, rw=rw)

        attn = _attention(*qkv)

        rec = _rec(xr, gr, rec_conv_w[layer],
                   rec_conv_b[layer].reshape(1, rw), _block_diag(w_rg[layer]).astype(BF16),
                   b_rg[layer].reshape(1, rw), _block_diag(w_ig[layer]).astype(BF16),
                   b_ig[layer].reshape(1, rw), lru_lambda[layer].reshape(1, rw),
                   g_rec_out[layer].reshape(1, rw))

        w_out_b = w_out[layer].astype(BF16)
        x = _ffn(x, attn, rec, g_attn_out[layer].reshape(1, aw), w_out_b,
                 g_ffn[layer].reshape(1, d), w_up[layer].astype(BF16),
                 ffn_conv_w[layer], ffn_conv_b[layer].reshape(1, -1), w_down[layer].astype(BF16))
    return x
```

```python
import functools

import jax
import jax.numpy as jnp
import numpy as np
from jax import lax
from jax.experimental import pallas as pl
from jax.experimental.pallas import tpu as pltpu

HEAD_DIM = 64
REC_CONV = 4
LRU_C = 8.0
FFN_CONV = 3
WINDOW_DILATIONS = ((128, 1), (512, 4), (2048, 16))
DIL_MID, DIL_FAR = 4, 16
BLOCK = 128
ROPE_THETA = 10000.0
EPS = 1e-6
NEG_INF = -1e30

LANES = 128
SUBLANES = 8
VMEM_LIMIT = 56 * 1024 * 1024

F32 = jnp.float32
BF16 = jnp.bfloat16


def _cparams(sem, flags=None):
    return pltpu.CompilerParams(dimension_semantics=sem, vmem_limit_bytes=VMEM_LIMIT, flags=flags)


_GELU_C = float(np.float32(np.sqrt(2.0 / np.pi)))
_GELU_CK = _GELU_C * float(np.float32(0.044715))


def _gelu_times(g, other):
    t = jnp.tanh(g * (_GELU_C + _GELU_CK * (g * g)))
    half = (0.5 * g) * other
    return half + half * t


def _const_spec(shape):
    nd = len(shape)
    return pl.BlockSpec(shape, lambda *_: (0,) * nd)


def _inproj_kernel(x_ref, pos_ref, gmix_ref, w_ref, qg_ref, kg_ref, invf_ref,
                   q1, k1, v1, q4, k4, v4, q16, k16, v16, xr_ref, gr_ref, proj, s1, s4, *, aw, rw, nsub):
    tm = x_ref.shape[0]
    rows_per = tm // nsub
    nslab = aw // LANES
    lane = lax.broadcasted_iota(jnp.int32, (rows_per, LANES), 1)
    head_a = (lane & (HEAD_DIM // 2)) == 0
    n4 = rows_per // DIL_MID
    n16 = rows_per // DIL_FAR

    def matmuls(sub):
        rows = slice(sub * rows_per, (sub + 1) * rows_per)
        x = x_ref[rows, :]
        ms = jnp.mean(x * x, axis=-1, keepdims=True)
        h = (x * lax.rsqrt(ms + EPS) * gmix_ref[...]).astype(BF16)
        proj[sub % 2] = jnp.dot(h, w_ref[:, 0:3 * aw], preferred_element_type=F32)
        xr_ref[rows, :] = jnp.dot(h, w_ref[:, 3 * aw:3 * aw + rw], preferred_element_type=F32)
        gr_ref[rows, :] = jnp.dot(h, w_ref[:, 3 * aw + rw:3 * aw + 2 * rw], preferred_element_type=F32)

    def epilogue(sub):
        rows = slice(sub * rows_per, (sub + 1) * rows_per)
        slot = sub % 2
        ang = pos_ref[rows, :] * invf_ref[...]
        cos = jnp.cos(ang)
        sin = jnp.sin(ang)
        sin_signed = jnp.where(lane < HEAD_DIM, -sin, sin)

        def head_norm_rotary(c0, g_ref, scale):
            outs = []
            for g in range(nslab):
                cols = slice(g * LANES, (g + 1) * LANES)
                pg = proj[slot, :, c0 + g * LANES:c0 + (g + 1) * LANES]
                sq = pg * pg
                s_a = jnp.sum(jnp.where(head_a, sq, 0.0), axis=-1, keepdims=True)
                s_b = jnp.sum(jnp.where(head_a, 0.0, sq), axis=-1, keepdims=True)
                ms_h = jnp.where(head_a, s_a, s_b) * (1.0 / HEAD_DIM)
                xg = pg * lax.rsqrt(ms_h + EPS) * (g_ref[:, cols] * scale)
                outs.append(xg * cos + pltpu.roll(xg, HEAD_DIM, 1) * sin_signed)
            return outs

        def emit(which, slabs, o1, o4, o16):
            for g, val in enumerate(slabs):
                cols = slice(g * LANES, (g + 1) * LANES)
                o1[rows, cols] = val.astype(BF16)
                s1[slot, which, g] = val
                for c in range(DIL_MID):
                    t4 = s1[slot, which, g, pl.ds(c, n4, stride=DIL_MID), :]
                    o4[c, sub * n4:(sub + 1) * n4, cols] = t4.astype(BF16)
                    s4[slot, which, g, c * n4:(c + 1) * n4, :] = t4
                for c4 in range(DIL_MID):
                    for cp in range(DIL_FAR // DIL_MID):
                        t16 = s4[slot, which, g, pl.ds(c4 * n4 + cp, n16, stride=DIL_MID), :]
                        o16[c4 + DIL_MID * cp, sub * n16:(sub + 1) * n16, cols] = t16.astype(BF16)

        emit(0, head_norm_rotary(0, qg_ref, HEAD_DIM ** -0.5), q1, q4, q16)
        emit(1, head_norm_rotary(aw, kg_ref, 1.0), k1, k4, k16)
        emit(2, [proj[slot, :, 2 * aw + g * LANES:2 * aw + (g + 1) * LANES] for g in range(nslab)], v1, v4, v16)

    matmuls(0)
    for sub in range(nsub):
        if sub + 1 < nsub:
            matmuls(sub + 1)
        epilogue(sub)


def _inproj(x, pos, g_mix, w_in_b, qg, kg, invf, *, aw, rw, tm=1024, nsub=4):
    b, s, d = x.shape
    rows_per = tm // nsub
    assert rows_per % (2 * SUBLANES * DIL_FAR) == 0
    stage = pltpu.VMEM((2, 3, aw // LANES, rows_per, LANES), F32)
    row = lambda w: pl.BlockSpec((None, tm, w), lambda bi, i: (bi, i, 0))
    grp = lambda dil: pl.BlockSpec((None, dil, tm // dil, aw), lambda bi, i: (bi, 0, i, 0))
    grp_shape = lambda dil: jax.ShapeDtypeStruct((b, dil, s // dil, aw), BF16)
    tok_shape = jax.ShapeDtypeStruct((b, s, aw), BF16)
    return pl.pallas_call(
        functools.partial(_inproj_kernel, aw=aw, rw=rw, nsub=nsub),
        grid=(b, s // tm),
        in_specs=[row(d), row(1), _const_spec((1, d)), _const_spec(w_in_b.shape),
                  _const_spec((1, aw)), _const_spec((1, aw)),
                  _const_spec((1, LANES))],
        out_specs=[row(aw)] * 3 + [grp(DIL_MID)] * 3 + [grp(DIL_FAR)] * 3 + [row(rw)] * 2,
        out_shape=[tok_shape] * 3 + [grp_shape(DIL_MID)] * 3 + [grp_shape(DIL_FAR)] * 3
                  + [jax.ShapeDtypeStruct((b, s, rw), F32)] * 2,
        scratch_shapes=[pltpu.VMEM((2, rows_per, 3 * aw), F32), stage, stage],
        compiler_params=_cparams(("parallel", "parallel")),
        name="inproj",
    )(x, pos, g_mix, w_in_b, qg, kg, invf)


GROUP_BLOCKS = 8


def _attn_kernel(q1, k1, v1, q4, k4, v4, q16, k16, v16, out_ref,
                 va1, vb1, va4, vb4, va16, vb16, acc_s, m_s, l_s, tmp_a, tmp_m, tmp_l,
                 bias_s, s_a, s_b, m_a, m_b, *, span):
    seq = out_ref.shape[0]
    grp = seq // DIL_MID
    gb = GROUP_BLOCKS
    lane = lax.broadcasted_iota(jnp.int32, (BLOCK, LANES), 1)
    low_head = lane < HEAD_DIM
    q_head_a = (lane & (HEAD_DIM // 2)) == 0
    q_head_b = jnp.logical_not(q_head_a)

    qi = lax.broadcasted_iota(jnp.int32, (BLOCK, 2 * BLOCK), 0)
    kj = lax.broadcasted_iota(jnp.int32, (BLOCK, 2 * BLOCK), 1)
    for e in range(2):
        rel = qi - kj + e * BLOCK
        bias_s[e] = jnp.where((rel >= 0) & (rel <= span), 0.0, NEG_INF).astype(F32)

    for v_ref, va, vb in ((v1, va1, vb1), (v4, va4, vb4), (v16, va16, vb16)):
        for c in range(v_ref.shape[0]):
            vv = v_ref[c]
            lo = lax.broadcasted_iota(jnp.int32, vv.shape, 1) < HEAD_DIM
            one = jnp.ones_like(vv)
            va[c] = jnp.where(lo, vv, one)
            vb[c] = jnp.where(lo, one, vv)

    def rows_of(i):
        return i * BLOCK, max(i - 1, 0) * BLOCK, min(i, 1)

    def scores(group, bufs):
        q_ref, k_ref, _, _, coords, _ = group
        s_buf, m_buf = bufs
        for t, (c, i) in enumerate(coords):
            r0, start, e = rows_of(i)
            qp = q_ref[c, r0:r0 + BLOCK, :]
            ks = k_ref[c, start:start + 2 * BLOCK, :]
            for h, sel in enumerate((q_head_a, q_head_b)):
                qm = jnp.where(sel, qp, jnp.zeros_like(qp))
                s = lax.dot_general(qm, ks, (((1,), (1,)), ((), ())), preferred_element_type=F32)
                s = s + bias_s[e]
                s_buf[2 * t + h] = s
                m = jnp.max(jnp.maximum(s[:, 0:BLOCK], s[:, BLOCK:]), axis=-1, keepdims=True)
                m_buf[2 * t + h] = jnp.broadcast_to(m, (BLOCK, LANES))

    def finish(group, bufs):
        _, _, va, vb, coords, consume = group
        s_buf, m_buf = bufs
        for t, (c, i) in enumerate(coords):
            _, start, _ = rows_of(i)
            pvs = []
            for h, v_ref in enumerate((va, vb)):
                hb = 2 * t + h
                m = m_buf[hb]
                p = jnp.concatenate([jnp.exp(s_buf[hb, :, 0:BLOCK] - m).astype(BF16),
                                     jnp.exp(s_buf[hb, :, BLOCK:] - m).astype(BF16)], axis=1)
                pvs.append(jnp.dot(p, v_ref[c, start:start + 2 * BLOCK, :], preferred_element_type=F32))
            acc = jnp.where(low_head, pvs[0], pvs[1])
            l = pltpu.roll(jnp.where(low_head, pvs[1], pvs[0]), HEAD_DIM, 1)
            m = jnp.where(low_head, m_buf[2 * t], m_buf[2 * t + 1])
            consume(c, i, (acc, m, l))

    def merge(old, new):
        acc0, m0, l0 = old
        acc1, m1, l1 = new
        mn = jnp.maximum(m0, m1)
        e0 = jnp.exp(m0 - mn)
        e1 = jnp.exp(m1 - mn)
        return e0 * acc0 + e1 * acc1, mn, e0 * l0 + e1 * l1

    def mid_store(c, i, new):
        r0 = c * grp + i * BLOCK
        for ref, val in zip((acc_s, m_s, l_s), new):
            ref[r0:r0 + BLOCK, :] = val

    ratio = DIL_FAR // DIL_MID
    far_blocks = seq // DIL_FAR // BLOCK

    def far_merge(c, i, new):
        row0 = (c % DIL_MID) * grp + c // DIL_MID + i * (BLOCK * ratio)
        rows = pl.ds(row0, BLOCK, stride=ratio)
        merged = merge((acc_s[rows, :], m_s[rows, :], l_s[rows, :]), new)
        for ref, val in zip((acc_s, m_s, l_s), merged):
            ref[rows, :] = val

    def near_emit(c, i, new):
        per = BLOCK // DIL_MID
        for r in range(DIL_MID):
            src = slice(r * grp + i * per, r * grp + (i + 1) * per)
            dst = pl.ds(r, per, stride=DIL_MID)
            tmp_a[dst, :] = acc_s[src, :]
            tmp_m[dst, :] = m_s[src, :]
            tmp_l[dst, :] = l_s[src, :]
        acc, _, l = merge((tmp_a[...], tmp_m[...], tmp_l[...]), new)
        out_ref[i * BLOCK:(i + 1) * BLOCK, :] = (acc / l).astype(out_ref.dtype)

    groups = [(q4, k4, va4, vb4, [(c, t) for t in range(gb)], mid_store) for c in range(DIL_MID)]
    groups += [(q16, k16, va16, vb16,
                [(gi * (gb // far_blocks) + t // far_blocks, t % far_blocks) for t in range(gb)], far_merge)
               for gi in range(DIL_FAR * far_blocks // gb)]
    groups += [(q1, k1, va1, vb1, [(0, gi * gb + t) for t in range(gb)], near_emit)
               for gi in range(seq // BLOCK // gb)]

    s_bufs = ((s_a, m_a), (s_b, m_b))
    scores(groups[0], s_bufs[0])
    for k, group in enumerate(groups):
        if k + 1 < len(groups):
            scores(groups[k + 1], s_bufs[(k + 1) % 2])
        finish(group, s_bufs[k % 2])


def _attention(q1, k1, v1, q4, k4, v4, q16, k16, v16):
    b, s, w = q1.shape
    assert WINDOW_DILATIONS == ((BLOCK, 1), (BLOCK * DIL_MID, DIL_MID), (BLOCK * DIL_FAR, DIL_FAR))
    assert s % (BLOCK * DIL_FAR) == 0 and s // DIL_FAR >= 2 * BLOCK
    gb = GROUP_BLOCKS
    assert s // DIL_MID // BLOCK == gb and gb % (s // DIL_FAR // BLOCK) == 0 and (s // BLOCK) % gb == 0
    tok4 = lambda t: t.reshape(b, 1, s, w)
    spec = lambda dil: pl.BlockSpec((None, dil, s // dil, LANES), lambda bi, g: (bi, 0, 0, g))
    vshape = lambda dil: pltpu.VMEM((dil, s // dil, LANES), BF16)
    return pl.pallas_call(
        functools.partial(_attn_kernel, span=BLOCK),
        grid=(b, w // LANES),
        in_specs=[spec(1)] * 3 + [spec(DIL_MID)] * 3 + [spec(DIL_FAR)] * 3,
        out_specs=pl.BlockSpec((None, s, LANES), lambda bi, g: (bi, 0, g)),
        out_shape=jax.ShapeDtypeStruct((b, s, w), BF16),
        scratch_shapes=[vshape(1), vshape(1), vshape(DIL_MID), vshape(DIL_MID), vshape(DIL_FAR),
                        vshape(DIL_FAR)]
                       + [pltpu.VMEM((s, LANES), F32)] * 3 + [pltpu.VMEM((BLOCK, LANES), F32)] * 3
                       + [pltpu.VMEM((2, BLOCK, 2 * BLOCK), F32),
                          pltpu.VMEM((2 * gb, BLOCK, 2 * BLOCK), F32),
                          pltpu.VMEM((2 * gb, BLOCK, 2 * BLOCK), F32),
                          pltpu.VMEM((2 * gb, BLOCK, LANES), F32),
                          pltpu.VMEM((2 * gb, BLOCK, LANES), F32)],
        compiler_params=_cparams(("parallel", "parallel")),
        name="attn",
    )(tok4(q1), tok4(k1), tok4(v1), q4, k4, v4, q16, k16, v16)


def _rec_kernel(xr_ref, gr_ref, cw_ref, cb_ref, wr_ref, br_ref, wi_ref, bi_ref, lam_ref, g_ref,
                out_ref, xe, a_s, h_s, carry, *, pitch):
    ts = xr_ref.shape[0]
    pad = SUBLANES

    @pl.when(pl.program_id(1) == 0)
    def _():
        xe[0:pad, :] = jnp.zeros((pad, xe.shape[1]), F32)
        carry[...] = jnp.zeros_like(carry)

    x = xr_ref[...]
    xe[pad:, :] = x
    xc = cb_ref[...] + cw_ref[REC_CONV - 1:REC_CONV, :] * x
    for kk in range(1, REC_CONV):
        xc = xc + cw_ref[REC_CONV - 1 - kk:REC_CONV - kk, :] * xe[pl.ds(pad - kk, ts), :]
    xe[0:pad, :] = x[ts - pad:, :]

    xb = xc.astype(BF16)
    r = jax.nn.sigmoid(jnp.dot(xb, wr_ref[...], preferred_element_type=F32) + br_ref[...])
    i = jax.nn.sigmoid(jnp.dot(xb, wi_ref[...], preferred_element_type=F32) + bi_ref[...])
    z = -lam_ref[...]
    softplus = jnp.maximum(z, 0.0) + jnp.log1p(jnp.exp(-jnp.abs(z)))
    log_a = (-LRU_C * softplus) * r
    a = jnp.exp(log_a)
    w = jnp.tanh(-log_a) * (a * a + 1.0)
    u = jnp.where(w > 0.0, w * lax.rsqrt(w), 0.0) * (i * xc)

    nch = SUBLANES
    clen = ts // nch
    nslab = a_s.shape[0]
    for g in range(nslab):
        cols = slice(g * LANES, (g + 1) * LANES)
        for j in range(nch):
            a_s[g, j * pitch:j * pitch + clen, :] = a[j * clen:(j + 1) * clen, cols]
            h_s[g, j * pitch:j * pitch + clen, :] = u[j * clen:(j + 1) * clen, cols]

    def step(s, hp):
        hs, ps = hp
        rows = pl.ds(s, nch, stride=pitch)
        new_h, new_p = [], []
        for g in range(nslab):
            av = a_s[g, rows, :]
            hv = av * hs[g] + h_s[g, rows, :]
            pv = av * ps[g]
            h_s[g, rows, :] = hv
            a_s[g, rows, :] = pv
            new_h.append(hv)
            new_p.append(pv)
        return tuple(new_h), tuple(new_p)

    zero = jnp.zeros((nch, LANES), F32)
    h_end, p_end = lax.fori_loop(0, clen, step, ((zero,) * nslab, (zero + 1.0,) * nslab))

    carry_in = []
    for g in range(nslab):
        cols = slice(g * LANES, (g + 1) * LANES)
        cj = carry[:, cols]
        per_chunk = []
        for j in range(nch):
            per_chunk.append(cj)
            cj = h_end[g][j:j + 1, :] + p_end[g][j:j + 1, :] * cj
        carry[:, cols] = cj
        carry_in.append(per_chunk)

    for j in range(nch):
        rows = slice(j * pitch, j * pitch + clen)
        h = jnp.concatenate([h_s[g, rows, :] + a_s[g, rows, :] * carry_in[g][j] for g in range(nslab)], axis=1)
        y = _gelu_times(gr_ref[j * clen:(j + 1) * clen, :], h)
        ms = jnp.mean(y * y, axis=-1, keepdims=True)
        out_ref[j * clen:(j + 1) * clen, :] = (y * lax.rsqrt(ms + EPS) * g_ref[...]).astype(out_ref.dtype)


def _rec(xr, gr, cw, cb, wr_bd, br, wi_bd, bi, lam, g, *, ts=1024):
    b, s, w = xr.shape
    blk = pl.BlockSpec((None, ts, w), lambda bi_, t: (bi_, t, 0))
    vec = _const_spec((1, w))
    pitch = ts // SUBLANES + 4
    assert (ts // SUBLANES) % 8 == 0
    scan_buf = pltpu.VMEM((w // LANES, SUBLANES * pitch, LANES), F32)
    return pl.pallas_call(
        functools.partial(_rec_kernel, pitch=pitch),
        grid=(b, s // ts),
        in_specs=[blk, blk, _const_spec(cw.shape), vec, _const_spec(wr_bd.shape), vec,
                  _const_spec(wi_bd.shape), vec, vec, vec],
        out_specs=blk,
        out_shape=jax.ShapeDtypeStruct((b, s, w), BF16),
        scratch_shapes=[pltpu.VMEM((ts + SUBLANES, w), F32), scan_buf, scan_buf,
                        pltpu.VMEM((1, w), F32)],
        compiler_params=_cparams(("parallel", "arbitrary")),
        name="rec",
    )(xr, gr, cw, cb, wr_bd, br, wi_bd, bi, lam, g)


def _ffn_kernel(x_ref, attn_ref, rec_ref, ga_ref, wo_ref, g_ref, wup_ref, cw_ref, cb_ref, wdn_ref,
                out_ref, ua, ub, tail, acc, h_s, act_a, act_b, *, cf):
    tm = x_ref.shape[0]
    pad = SUBLANES
    d_ff = wdn_ref.shape[0]
    bounds = list(range(0, d_ff + 1, cf))
    nchunks = len(bounds) - 1
    ubufs = (ua, ub)
    acts = (act_a, act_b)

    @pl.when(pl.program_id(1) == 0)
    def _():
        tail[...] = jnp.zeros_like(tail)
        for buf in ubufs:
            buf[:, pad + tm:, :] = jnp.zeros((FFN_CONV, pad, buf.shape[2]), F32)

    attn = attn_ref[...].astype(F32)
    ms = jnp.mean(attn * attn, axis=-1, keepdims=True)
    aw = attn.shape[1]
    h_s[:, 0:aw] = (attn * lax.rsqrt(ms + EPS) * ga_ref[...]).astype(BF16)
    h_s[:, aw:] = rec_ref[...]
    x = x_ref[...] + jnp.dot(h_s[...], wo_ref[...], preferred_element_type=F32)
    ms = jnp.mean(x * x, axis=-1, keepdims=True)
    h_s[...] = (x * lax.rsqrt(ms + EPS) * g_ref[...]).astype(BF16)
    acc[...] = x

    def halves(j):
        lo, hi = bounds[j], bounds[j + 1]
        for half in range(2):
            yield slice(half * d_ff + lo, half * d_ff + hi), slice(half * cf, half * cf + hi - lo)

    def up(j):
        buf = ubufs[j % 2]
        for cols, dst in halves(j):
            u = jnp.dot(h_s[...], wup_ref[:, cols], preferred_element_type=F32)
            buf[0, pad:pad + tm, dst] = u
            for kk in range(1, FFN_CONV):
                buf[kk, pad:2 * pad, dst] = tail[kk - 1, :, cols]
                buf[kk, pad + kk:pad + kk + tm, dst] = u
                tail[kk - 1, :, cols] = buf[kk, pad + tm:2 * pad + tm, dst]

    def elementwise(j):
        buf = ubufs[j % 2]
        parts = []
        for cols, dst in halves(j):
            uc = cb_ref[:, cols]
            for kk in range(FFN_CONV):
                uc = uc + cw_ref[FFN_CONV - 1 - kk:FFN_CONV - kk, cols] * buf[kk, pad:pad + tm, dst]
            parts.append(uc)
        acts[j % 2][:, 0:bounds[j + 1] - bounds[j]] = _gelu_times(parts[0], parts[1]).astype(BF16)

    def down(j):
        lo, hi = bounds[j], bounds[j + 1]
        acc[...] += jnp.dot(acts[j % 2][:, 0:hi - lo], wdn_ref[lo:hi, :], preferred_element_type=F32)

    up(0)
    for j in range(nchunks + 1):
        if j + 1 < nchunks:
            up(j + 1)
        if j < nchunks:
            elementwise(j)
        if j >= 1:
            down(j - 1)
    out_ref[...] = acc[...]


def _ffn(x, attn, rec, ga, wo, g, wup_b, cw, cb, wdn_b, *, tm=512, cf=512):
    b, s, d = x.shape
    d_ff = wdn_b.shape[0]
    assert d_ff % cf == 0
    row = lambda w: pl.BlockSpec((None, tm, w), lambda bi, t: (bi, t, 0))
    blk = row(d)
    return pl.pallas_call(
        functools.partial(_ffn_kernel, cf=cf),
        grid=(b, s // tm),
        in_specs=[blk, row(attn.shape[-1]), row(rec.shape[-1]), _const_spec(ga.shape),
                  _const_spec(wo.shape),
                  _const_spec((1, d)), _const_spec(wup_b.shape), _const_spec(cw.shape),
                  _const_spec(cb.shape), _const_spec(wdn_b.shape)],
        out_specs=blk,
        out_shape=jax.ShapeDtypeStruct((b, s, d), F32),
        scratch_shapes=[pltpu.VMEM((FFN_CONV, tm + 2 * SUBLANES, 2 * cf), F32)] * 2
                       + [pltpu.VMEM((FFN_CONV - 1, SUBLANES, 2 * d_ff), F32), pltpu.VMEM((tm, d), F32),
                          pltpu.VMEM((tm, d), BF16), pltpu.VMEM((tm, cf), BF16),
                          pltpu.VMEM((tm, cf), BF16)],
        compiler_params=_cparams(("parallel", "arbitrary")),
        name="ffn",
    )(x, attn, rec, ga, wo, g, wup_b, cw, cb, wdn_b)


def _block_diag(w):
    n, c, _ = w.shape
    eye = jnp.eye(n, dtype=w.dtype)
    return (eye[:, None, :, None] * w[:, :, None, :]).reshape(n * c, n * c)


def _pair_rotary_layout(a):
    lead = a.shape[:-1]
    half = HEAD_DIM // 2
    a = a.reshape(*lead, -1, 2, 2, half)
    return jnp.swapaxes(a, -3, -2).reshape(*lead, -1)


def kernel(x, positions, g_mix, w_in, q_norm_g, k_norm_g, rec_conv_w, rec_conv_b, w_rg, b_rg, w_ig,
           b_ig, lru_lambda, g_attn_out, g_rec_out, w_out, g_ffn, w_up, ffn_conv_w, ffn_conv_b, w_down):
    bsz, s, d = x.shape
    t = bsz * s
    depth = w_in.shape[0]
    rw = rec_conv_w.shape[-1]
    aw = w_out.shape[1] - rw
    n_heads = aw // HEAD_DIM

    half = HEAD_DIM // 2
    inv_freq = ROPE_THETA ** (-jnp.arange(half, dtype=F32) / half)
    invf = jnp.tile(inv_freq, LANES // half).reshape(1, LANES)
    pos = positions.astype(F32).reshape(bsz, s, 1)

    for layer in range(depth):
        w_l = w_in[layer].astype(BF16)
        w_l = jnp.concatenate([_pair_rotary_layout(w_l[:, :aw]), _pair_rotary_layout(w_l[:, aw:2 * aw]),
                               w_l[:, 2 * aw:]], axis=1)
        *qkv, xr, gr = _inproj(
            x, pos, g_mix[layer].reshape(1, d), w_l,
            _pair_rotary_layout(jnp.tile(q_norm_g[layer], n_heads)).reshape(1, aw),
            _pair_rotary_layout(jnp.tile(k_norm_g[layer], n_heads)).reshape(1, aw), invf, aw=aw, rw=rw)

        attn = _attention(*qkv)

        rec = _rec(xr, gr, rec_conv_w[layer],
                   rec_conv_b[layer].reshape(1, rw), _block_diag(w_rg[layer]).astype(BF16),
                   b_rg[layer].reshape(1, rw), _block_diag(w_ig[layer]).astype(BF16),
                   b_ig[layer].reshape(1, rw), lru_lambda[layer].reshape(1, rw),
                   g_rec_out[layer].reshape(1, rw))

        w_out_b = w_out[layer].astype(BF16)
        x = _ffn(x, attn, rec, g_attn_out[layer].reshape(1, aw), w_out_b,
                 g_ffn[layer].reshape(1, d), w_up[layer].astype(BF16),
                 ffn_conv_w[layer], ffn_conv_b[layer].reshape(1, -1), w_down[layer].astype(BF16))
    return x
```

```python
import functools

import jax
import jax.numpy as jnp
import numpy as np
from jax import lax
from jax.experimental import pallas as pl
from jax.experimental.pallas import tpu as pltpu

HEAD_DIM = 64
REC_CONV = 4
LRU_C = 8.0
FFN_CONV = 3
WINDOW_DILATIONS = ((128, 1), (512, 4), (2048, 16))
DIL_MID, DIL_FAR = 4, 16
BLOCK = 128
ROPE_THETA = 10000.0
EPS = 1e-6
NEG_INF = -1e30

LANES = 128
SUBLANES = 8
VMEM_LIMIT = 56 * 1024 * 1024

F32 = jnp.float32
BF16 = jnp.bfloat16


def _cparams(sem, flags=None):
    return pltpu.CompilerParams(dimension_semantics=sem, vmem_limit_bytes=VMEM_LIMIT, flags=flags)


_GELU_C = float(np.float32(np.sqrt(2.0 / np.pi)))
_GELU_CK = _GELU_C * float(np.float32(0.044715))


def _gelu_times(g, other):
    t = jnp.tanh(g * (_GELU_C + _GELU_CK * (g * g)))
    half = (0.5 * g) * other
    return half + half * t


def _const_spec(shape):
    nd = len(shape)
    return pl.BlockSpec(shape, lambda *_: (0,) * nd)


def _inproj_kernel(x_ref, pos_ref, gmix_ref, w_ref, qg_ref, kg_ref, invf_ref,
                   q1, k1, v1, q4, k4, v4, q16, k16, v16, xr_ref, gr_ref, proj, s1, s4, *, aw, rw, nsub):
    tm = x_ref.shape[0]
    rows_per = tm // nsub
    nslab = aw // LANES
    lane = lax.broadcasted_iota(jnp.int32, (rows_per, LANES), 1)
    head_a = lane < HEAD_DIM
    first_half = (lane & (HEAD_DIM // 2)) == 0
    n4 = rows_per // DIL_MID
    n16 = rows_per // DIL_FAR

    def matmuls(sub):
        rows = slice(sub * rows_per, (sub + 1) * rows_per)
        x = x_ref[rows, :]
        ms = jnp.mean(x * x, axis=-1, keepdims=True)
        h = (x * lax.rsqrt(ms + EPS) * gmix_ref[...]).astype(BF16)
        proj[sub % 2] = jnp.dot(h, w_ref[:, 0:3 * aw], preferred_element_type=F32)
        xr_ref[rows, :] = jnp.dot(h, w_ref[:, 3 * aw:3 * aw + rw], preferred_element_type=F32)
        gr_ref[rows, :] = jnp.dot(h, w_ref[:, 3 * aw + rw:3 * aw + 2 * rw], preferred_element_type=F32)

    def epilogue(sub):
        rows = slice(sub * rows_per, (sub + 1) * rows_per)
        slot = sub % 2
        pos_cols = [jnp.broadcast_to(pos_ref[r:r + 1, :], (LANES, LANES)).T
                    for r in range(sub * rows_per // LANES, (sub + 1) * rows_per // LANES)]
        ang = jnp.concatenate(pos_cols, axis=0) * invf_ref[...]
        cos = jnp.cos(ang)
        sin = jnp.sin(ang)
        sin_signed = jnp.where(first_half, -sin, sin)

        def head_norm_rotary(c0, g_ref, scale):
            outs = []
            for g in range(nslab):
                cols = slice(g * LANES, (g + 1) * LANES)
                pg = proj[slot, :, c0 + g * LANES:c0 + (g + 1) * LANES]
                sq = pg * pg
                s_a = jnp.sum(jnp.where(head_a, sq, 0.0), axis=-1, keepdims=True)
                s_b = jnp.sum(jnp.where(head_a, 0.0, sq), axis=-1, keepdims=True)
                ms_h = jnp.where(head_a, s_a, s_b) * (1.0 / HEAD_DIM)
                xg = pg * lax.rsqrt(ms_h + EPS) * (g_ref[:, cols] * scale)
                swapped = jnp.where(first_half, pltpu.roll(xg, LANES - HEAD_DIM // 2, 1),
                                    pltpu.roll(xg, HEAD_DIM // 2, 1))
                outs.append(xg * cos + swapped * sin_signed)
            return outs

        def emit(which, slabs, o1, o4, o16):
            for g, val in enumerate(slabs):
                cols = slice(g * LANES, (g + 1) * LANES)
                o1[rows, cols] = val.astype(BF16)
                s1[slot, which, g] = val
                for c in range(DIL_MID):
                    t4 = s1[slot, which, g, pl.ds(c, n4, stride=DIL_MID), :]
                    o4[c, sub * n4:(sub + 1) * n4, cols] = t4.astype(BF16)
                    s4[slot, which, g, c * n4:(c + 1) * n4, :] = t4
                for c4 in range(DIL_MID):
                    for cp in range(DIL_FAR // DIL_MID):
                        t16 = s4[slot, which, g, pl.ds(c4 * n4 + cp, n16, stride=DIL_MID), :]
                        o16[c4 + DIL_MID * cp, sub * n16:(sub + 1) * n16, cols] = t16.astype(BF16)

        emit(0, head_norm_rotary(0, qg_ref, HEAD_DIM ** -0.5), q1, q4, q16)
        emit(1, head_norm_rotary(aw, kg_ref, 1.0), k1, k4, k16)
        emit(2, [proj[slot, :, 2 * aw + g * LANES:2 * aw + (g + 1) * LANES] for g in range(nslab)], v1, v4, v16)

    matmuls(0)
    for sub in range(nsub):
        if sub + 1 < nsub:
            matmuls(sub + 1)
        epilogue(sub)


def _inproj(x, pos, g_mix, w_in_b, qg, kg, invf, *, aw, rw, tm=1024, nsub=4):
    b, s, d = x.shape
    rows_per = tm // nsub
    assert rows_per % (2 * SUBLANES * DIL_FAR) == 0
    stage = pltpu.VMEM((2, 3, aw // LANES, rows_per, LANES), F32)
    row = lambda w: pl.BlockSpec((None, tm, w), lambda bi, i: (bi, i, 0))
    grp = lambda dil: pl.BlockSpec((None, dil, tm // dil, aw), lambda bi, i: (bi, 0, i, 0))
    grp_shape = lambda dil: jax.ShapeDtypeStruct((b, dil, s // dil, aw), BF16)
    tok_shape = jax.ShapeDtypeStruct((b, s, aw), BF16)
    return pl.pallas_call(
        functools.partial(_inproj_kernel, aw=aw, rw=rw, nsub=nsub),
        grid=(b, s // tm),
        in_specs=[row(d), pl.BlockSpec((None, tm // LANES, LANES), lambda bi, i: (bi, i, 0)),
                  _const_spec((1, d)), _const_spec(w_in_b.shape),
                  _const_spec((1, aw)), _const_spec((1, aw)),
                  _const_spec((1, LANES))],
        out_specs=[row(aw)] * 3 + [grp(DIL_MID)] * 3 + [grp(DIL_FAR)] * 3 + [row(rw)] * 2,
        out_shape=[tok_shape] * 3 + [grp_shape(DIL_MID)] * 3 + [grp_shape(DIL_FAR)] * 3
                  + [jax.ShapeDtypeStruct((b, s, rw), F32)] * 2,
        scratch_shapes=[pltpu.VMEM((2, rows_per, 3 * aw), F32), stage, stage],
        compiler_params=_cparams(("parallel", "parallel")),
        name="inproj",
    )(x, pos, g_mix, w_in_b, qg, kg, invf)


GROUP_BLOCKS = 8


def _attn_kernel(q1, k1, v1, q4, k4, v4, q16, k16, v16, out_ref,
                 va1, vb1, va4, vb4, va16, vb16, acc_s, m_s, l_s, tmp_a, tmp_m, tmp_l,
                 bias_s, s_a, s_b, m_a, m_b, *, span):
    seq = out_ref.shape[0]
    grp = seq // DIL_MID
    gb = GROUP_BLOCKS
    lane = lax.broadcasted_iota(jnp.int32, (BLOCK, LANES), 1)
    low_head = lane < HEAD_DIM
    high_head = jnp.logical_not(low_head)

    qi = lax.broadcasted_iota(jnp.int32, (BLOCK, 2 * BLOCK), 0)
    kj = lax.broadcasted_iota(jnp.int32, (BLOCK, 2 * BLOCK), 1)
    for e in range(2):
        rel = qi - kj + e * BLOCK
        bias_s[e] = jnp.where((rel >= 0) & (rel <= span), 0.0, NEG_INF).astype(F32)

    for v_ref, va, vb in ((v1, va1, vb1), (v4, va4, vb4), (v16, va16, vb16)):
        for c in range(v_ref.shape[0]):
            vv = v_ref[c]
            lo = lax.broadcasted_iota(jnp.int32, vv.shape, 1) < HEAD_DIM
            one = jnp.ones_like(vv)
            va[c] = jnp.where(lo, vv, one)
            vb[c] = jnp.where(lo, one, vv)

    def rows_of(i):
        return i * BLOCK, max(i - 1, 0) * BLOCK, min(i, 1)

    def scores(group, bufs):
        q_ref, k_ref, _, _, coords, _ = group
        s_buf, m_buf = bufs
        for t, (c, i) in enumerate(coords):
            r0, start, e = rows_of(i)
            qp = q_ref[c, r0:r0 + BLOCK, :]
            ks = k_ref[c, start:start + 2 * BLOCK, :]
            for h, sel in enumerate((low_head, high_head)):
                qm = jnp.where(sel, qp, jnp.zeros_like(qp))
                s = lax.dot_general(qm, ks, (((1,), (1,)), ((), ())), preferred_element_type=F32)
                s = s + bias_s[e]
                s_buf[2 * t + h] = s
                m = jnp.max(jnp.maximum(s[:, 0:BLOCK], s[:, BLOCK:]), axis=-1, keepdims=True)
                m_buf[2 * t + h] = jnp.broadcast_to(m, (BLOCK, LANES))

    def finish(group, bufs):
        _, _, va, vb, coords, consume = group
        s_buf, m_buf = bufs
        for t, (c, i) in enumerate(coords):
            _, start, _ = rows_of(i)
            pvs = []
            for h, v_ref in enumerate((va, vb)):
                hb = 2 * t + h
                m = m_buf[hb]
                p = jnp.concatenate([jnp.exp(s_buf[hb, :, 0:BLOCK] - m).astype(BF16),
                                     jnp.exp(s_buf[hb, :, BLOCK:] - m).astype(BF16)], axis=1)
                pvs.append(jnp.dot(p, v_ref[c, start:start + 2 * BLOCK, :], preferred_element_type=F32))
            acc = jnp.where(low_head, pvs[0], pvs[1])
            l = pltpu.roll(jnp.where(low_head, pvs[1], pvs[0]), HEAD_DIM, 1)
            m = jnp.where(low_head, m_buf[2 * t], m_buf[2 * t + 1])
            consume(c, i, (acc, m, l))

    def merge(old, new):
        acc0, m0, l0 = old
        acc1, m1, l1 = new
        mn = jnp.maximum(m0, m1)
        e0 = jnp.exp(m0 - mn)
        e1 = jnp.exp(m1 - mn)
        return e0 * acc0 + e1 * acc1, mn, e0 * l0 + e1 * l1

    def mid_store(c, i, new):
        r0 = c * grp + i * BLOCK
        for ref, val in zip((acc_s, m_s, l_s), new):
            ref[r0:r0 + BLOCK, :] = val

    ratio = DIL_FAR // DIL_MID
    far_blocks = seq // DIL_FAR // BLOCK

    def far_merge(c, i, new):
        row0 = (c % DIL_MID) * grp + c // DIL_MID + i * (BLOCK * ratio)
        rows = pl.ds(row0, BLOCK, stride=ratio)
        merged = merge((acc_s[rows, :], m_s[rows, :], l_s[rows, :]), new)
        for ref, val in zip((acc_s, m_s, l_s), merged):
            ref[rows, :] = val

    def near_emit(c, i, new):
        per = BLOCK // DIL_MID
        for r in range(DIL_MID):
            src = slice(r * grp + i * per, r * grp + (i + 1) * per)
            dst = pl.ds(r, per, stride=DIL_MID)
            tmp_a[dst, :] = acc_s[src, :]
            tmp_m[dst, :] = m_s[src, :]
            tmp_l[dst, :] = l_s[src, :]
        acc, _, l = merge((tmp_a[...], tmp_m[...], tmp_l[...]), new)
        out_ref[i * BLOCK:(i + 1) * BLOCK, :] = (acc / l).astype(out_ref.dtype)

    groups = [(q4, k4, va4, vb4, [(c, t) for t in range(gb)], mid_store) for c in range(DIL_MID)]
    groups += [(q16, k16, va16, vb16,
                [(gi * (gb // far_blocks) + t // far_blocks, t % far_blocks) for t in range(gb)], far_merge)
               for gi in range(DIL_FAR * far_blocks // gb)]
    groups += [(q1, k1, va1, vb1, [(0, gi * gb + t) for t in range(gb)], near_emit)
               for gi in range(seq // BLOCK // gb)]

    s_bufs = ((s_a, m_a), (s_b, m_b))
    scores(groups[0], s_bufs[0])
    for k, group in enumerate(groups):
        if k + 1 < len(groups):
            scores(groups[k + 1], s_bufs[(k + 1) % 2])
        finish(group, s_bufs[k % 2])


def _attention(q1, k1, v1, q4, k4, v4, q16, k16, v16):
    b, s, w = q1.shape
    assert WINDOW_DILATIONS == ((BLOCK, 1), (BLOCK * DIL_MID, DIL_MID), (BLOCK * DIL_FAR, DIL_FAR))
    assert s % (BLOCK * DIL_FAR) == 0 and s // DIL_FAR >= 2 * BLOCK
    gb = GROUP_BLOCKS
    assert s // DIL_MID // BLOCK == gb and gb % (s // DIL_FAR // BLOCK) == 0 and (s // BLOCK) % gb == 0
    tok4 = lambda t: t.reshape(b, 1, s, w)
    spec = lambda dil: pl.BlockSpec((None, dil, s // dil, LANES), lambda bi, g: (bi, 0, 0, g))
    vshape = lambda dil: pltpu.VMEM((dil, s // dil, LANES), BF16)
    return pl.pallas_call(
        functools.partial(_attn_kernel, span=BLOCK),
        grid=(b, w // LANES),
        in_specs=[spec(1)] * 3 + [spec(DIL_MID)] * 3 + [spec(DIL_FAR)] * 3,
        out_specs=pl.BlockSpec((None, s, LANES), lambda bi, g: (bi, 0, g)),
        out_shape=jax.ShapeDtypeStruct((b, s, w), BF16),
        scratch_shapes=[vshape(1), vshape(1), vshape(DIL_MID), vshape(DIL_MID), vshape(DIL_FAR),
                        vshape(DIL_FAR)]
                       + [pltpu.VMEM((s, LANES), F32)] * 3 + [pltpu.VMEM((BLOCK, LANES), F32)] * 3
                       + [pltpu.VMEM((2, BLOCK, 2 * BLOCK), F32),
                          pltpu.VMEM((2 * gb, BLOCK, 2 * BLOCK), F32),
                          pltpu.VMEM((2 * gb, BLOCK, 2 * BLOCK), F32),
                          pltpu.VMEM((2 * gb, BLOCK, LANES), F32),
                          pltpu.VMEM((2 * gb, BLOCK, LANES), F32)],
        compiler_params=_cparams(("parallel", "parallel")),
        name="attn",
    )(tok4(q1), tok4(k1), tok4(v1), q4, k4, v4, q16, k16, v16)


def _rec_kernel(xr_ref, gr_ref, cw_ref, cb_ref, wr_ref, br_ref, wi_ref, bi_ref, lam_ref, g_ref,
                out_ref, xe, a_s, h_s, carry, *, pitch):
    ts = xr_ref.shape[0]
    pad = SUBLANES

    @pl.when(pl.program_id(1) == 0)
    def _():
        xe[0:pad, :] = jnp.zeros((pad, xe.shape[1]), F32)
        carry[...] = jnp.zeros_like(carry)

    x = xr_ref[...]
    xe[pad:, :] = x
    xc = cb_ref[...] + cw_ref[REC_CONV - 1:REC_CONV, :] * x
    for kk in range(1, REC_CONV):
        xc = xc + cw_ref[REC_CONV - 1 - kk:REC_CONV - kk, :] * xe[pl.ds(pad - kk, ts), :]
    xe[0:pad, :] = x[ts - pad:, :]

    xb = xc.astype(BF16)
    r = jax.nn.sigmoid(jnp.dot(xb, wr_ref[...], preferred_element_type=F32) + br_ref[...])
    i = jax.nn.sigmoid(jnp.dot(xb, wi_ref[...], preferred_element_type=F32) + bi_ref[...])
    z = -lam_ref[...]
    softplus = jnp.maximum(z, 0.0) + jnp.log1p(jnp.exp(-jnp.abs(z)))
    log_a = (-LRU_C * softplus) * r
    a = jnp.exp(log_a)
    w = jnp.tanh(-log_a) * (a * a + 1.0)
    u = jnp.where(w > 0.0, w * lax.rsqrt(w), 0.0) * (i * xc)

    nch = SUBLANES
    clen = ts // nch
    nslab = a_s.shape[0]
    for g in range(nslab):
        cols = slice(g * LANES, (g + 1) * LANES)
        for j in range(nch):
            a_s[g, j * pitch:j * pitch + clen, :] = a[j * clen:(j + 1) * clen, cols]
            h_s[g, j * pitch:j * pitch + clen, :] = u[j * clen:(j + 1) * clen, cols]

    def step(s, hp):
        hs, ps = hp
        rows = pl.ds(s, nch, stride=pitch)
        new_h, new_p = [], []
        for g in range(nslab):
            av = a_s[g, rows, :]
            hv = av * hs[g] + h_s[g, rows, :]
            pv = av * ps[g]
            h_s[g, rows, :] = hv
            a_s[g, rows, :] = pv
            new_h.append(hv)
            new_p.append(pv)
        return tuple(new_h), tuple(new_p)

    zero = jnp.zeros((nch, LANES), F32)
    h_end, p_end = lax.fori_loop(0, clen, step, ((zero,) * nslab, (zero + 1.0,) * nslab))

    carry_in = []
    for g in range(nslab):
        cols = slice(g * LANES, (g + 1) * LANES)
        cj = carry[:, cols]
        per_chunk = []
        for j in range(nch):
            per_chunk.append(cj)
            cj = h_end[g][j:j + 1, :] + p_end[g][j:j + 1, :] * cj
        carry[:, cols] = cj
        carry_in.append(per_chunk)

    for j in range(nch):
        rows = slice(j * pitch, j * pitch + clen)
        h = jnp.concatenate([h_s[g, rows, :] + a_s[g, rows, :] * carry_in[g][j] for g in range(nslab)], axis=1)
        y = _gelu_times(gr_ref[j * clen:(j + 1) * clen, :], h)
        ms = jnp.mean(y * y, axis=-1, keepdims=True)
        out_ref[j * clen:(j + 1) * clen, :] = (y * lax.rsqrt(ms + EPS) * g_ref[...]).astype(out_ref.dtype)


def _rec(xr, gr, cw, cb, wr_bd, br, wi_bd, bi, lam, g, *, ts=1024):
    b, s, w = xr.shape
    blk = pl.BlockSpec((None, ts, w), lambda bi_, t: (bi_, t, 0))
    vec = _const_spec((1, w))
    pitch = ts // SUBLANES + 4
    assert (ts // SUBLANES) % 8 == 0
    scan_buf = pltpu.VMEM((w // LANES, SUBLANES * pitch, LANES), F32)
    return pl.pallas_call(
        functools.partial(_rec_kernel, pitch=pitch),
        grid=(b, s // ts),
        in_specs=[blk, blk, _const_spec(cw.shape), vec, _const_spec(wr_bd.shape), vec,
                  _const_spec(wi_bd.shape), vec, vec, vec],
        out_specs=blk,
        out_shape=jax.ShapeDtypeStruct((b, s, w), BF16),
        scratch_shapes=[pltpu.VMEM((ts + SUBLANES, w), F32), scan_buf, scan_buf,
                        pltpu.VMEM((1, w), F32)],
        compiler_params=_cparams(("parallel", "arbitrary")),
        name="rec",
    )(xr, gr, cw, cb, wr_bd, br, wi_bd, bi, lam, g)


def _ffn_kernel(x_ref, attn_ref, rec_ref, ga_ref, wo_ref, g_ref, wup_ref, cw_ref, cb_ref, wdn_ref,
                out_ref, ua, ub, tail, acc, h_s, act_a, act_b, *, cf):
    tm = x_ref.shape[0]
    pad = SUBLANES
    d_ff = wdn_ref.shape[0]
    bounds = list(range(0, d_ff + 1, cf))
    nchunks = len(bounds) - 1
    ubufs = (ua, ub)
    acts = (act_a, act_b)

    @pl.when(pl.program_id(1) == 0)
    def _():
        tail[...] = jnp.zeros_like(tail)
        for buf in ubufs:
            buf[:, pad + tm:, :] = jnp.zeros((FFN_CONV, pad, buf.shape[2]), F32)

    attn = attn_ref[...].astype(F32)
    ms = jnp.mean(attn * attn, axis=-1, keepdims=True)
    aw = attn.shape[1]
    h_s[:, 0:aw] = (attn * lax.rsqrt(ms + EPS) * ga_ref[...]).astype(BF16)
    h_s[:, aw:] = rec_ref[...]
    x = x_ref[...] + jnp.dot(h_s[...], wo_ref[...], preferred_element_type=F32)
    ms = jnp.mean(x * x, axis=-1, keepdims=True)
    h_s[...] = (x * lax.rsqrt(ms + EPS) * g_ref[...]).astype(BF16)
    acc[...] = x

    def halves(j):
        lo, hi = bounds[j], bounds[j + 1]
        for half in range(2):
            yield slice(half * d_ff + lo, half * d_ff + hi), slice(half * cf, half * cf + hi - lo)

    def up(j):
        buf = ubufs[j % 2]
        for cols, dst in halves(j):
            u = jnp.dot(h_s[...], wup_ref[:, cols], preferred_element_type=F32)
            buf[0, pad:pad + tm, dst] = u
            for kk in range(1, FFN_CONV):
                buf[kk, pad:2 * pad, dst] = tail[kk - 1, :, cols]
                buf[kk, pad + kk:pad + kk + tm, dst] = u
                tail[kk - 1, :, cols] = buf[kk, pad + tm:2 * pad + tm, dst]

    def elementwise(j):
        buf = ubufs[j % 2]
        parts = []
        for cols, dst in halves(j):
            uc = cb_ref[:, cols]
            for kk in range(FFN_CONV):
                uc = uc + cw_ref[FFN_CONV - 1 - kk:FFN_CONV - kk, cols] * buf[kk, pad:pad + tm, dst]
            parts.append(uc)
        acts[j % 2][:, 0:bounds[j + 1] - bounds[j]] = _gelu_times(parts[0], parts[1]).astype(BF16)

    def down(j):
        lo, hi = bounds[j], bounds[j + 1]
        acc[...] += jnp.dot(acts[j % 2][:, 0:hi - lo], wdn_ref[lo:hi, :], preferred_element_type=F32)

    up(0)
    for j in range(nchunks + 1):
        if j + 1 < nchunks:
            up(j + 1)
        if j < nchunks:
            elementwise(j)
        if j >= 1:
            down(j - 1)
    out_ref[...] = acc[...]


def _ffn(x, attn, rec, ga, wo, g, wup_b, cw, cb, wdn_b, *, tm=512, cf=512):
    b, s, d = x.shape
    d_ff = wdn_b.shape[0]
    assert d_ff % cf == 0
    row = lambda w: pl.BlockSpec((None, tm, w), lambda bi, t: (bi, t, 0))
    blk = row(d)
    return pl.pallas_call(
        functools.partial(_ffn_kernel, cf=cf),
        grid=(b, s // tm),
        in_specs=[blk, row(attn.shape[-1]), row(rec.shape[-1]), _const_spec(ga.shape),
                  _const_spec(wo.shape),
                  _const_spec((1, d)), _const_spec(wup_b.shape), _const_spec(cw.shape),
                  _const_spec(cb.shape), _const_spec(wdn_b.shape)],
        out_specs=blk,
        out_shape=jax.ShapeDtypeStruct((b, s, d), F32),
        scratch_shapes=[pltpu.VMEM((FFN_CONV, tm + 2 * SUBLANES, 2 * cf), F32)] * 2
                       + [pltpu.VMEM((FFN_CONV - 1, SUBLANES, 2 * d_ff), F32), pltpu.VMEM((tm, d), F32),
                          pltpu.VMEM((tm, d), BF16), pltpu.VMEM((tm, cf), BF16),
                          pltpu.VMEM((tm, cf), BF16)],
        compiler_params=_cparams(("parallel", "arbitrary")),
        name="ffn",
    )(x, attn, rec, ga, wo, g, wup_b, cw, cb, wdn_b)


def _block_diag(w):
    n, c, _ = w.shape
    eye = jnp.eye(n, dtype=w.dtype)
    return (eye[:, None, :, None] * w[:, :, None, :]).reshape(n * c, n * c)


def kernel(x, positions, g_mix, w_in, q_norm_g, k_norm_g, rec_conv_w, rec_conv_b, w_rg, b_rg, w_ig,
           b_ig, lru_lambda, g_attn_out, g_rec_out, w_out, g_ffn, w_up, ffn_conv_w, ffn_conv_b, w_down):
    bsz, s, d = x.shape
    t = bsz * s
    depth = w_in.shape[0]
    rw = rec_conv_w.shape[-1]
    aw = w_out.shape[1] - rw
    n_heads = aw // HEAD_DIM

    half = HEAD_DIM // 2
    inv_freq = ROPE_THETA ** (-jnp.arange(half, dtype=F32) / half)
    invf = jnp.tile(inv_freq, LANES // half).reshape(1, LANES)
    pos = positions.astype(F32).reshape(bsz, s // LANES, LANES)

    for layer in range(depth):
        *qkv, xr, gr = _inproj(
            x, pos, g_mix[layer].reshape(1, d), w_in[layer].astype(BF16),
            jnp.tile(q_norm_g[layer], n_heads).reshape(1, aw),
            jnp.tile(k_norm_g[layer], n_heads).reshape(1, aw), invf, aw=aw, rw=rw)

        attn = _attention(*qkv)

        rec = _rec(xr, gr, rec_conv_w[layer],
                   rec_conv_b[layer].reshape(1, rw), _block_diag(w_rg[layer]).astype(BF16),
                   b_rg[layer].reshape(1, rw), _block_diag(w_ig[layer]).astype(BF16),
                   b_ig[layer].reshape(1, rw), lru_lambda[layer].reshape(1, rw),
                   g_rec_out[layer].reshape(1, rw))

        w_out_b = w_out[layer].astype(BF16)
        x = _ffn(x, attn, rec, g_attn_out[layer].reshape(1, aw), w_out_b,
                 g_ffn[layer].reshape(1, d), w_up[layer].astype(BF16),
                 ffn_conv_w[layer], ffn_conv_b[layer].reshape(1, -1), w_down[layer].astype(BF16))
    return x
```

```python
import functools

import jax
import jax.numpy as jnp
import numpy as np
from jax import lax
from jax.experimental import pallas as pl
from jax.experimental.pallas import tpu as pltpu

HEAD_DIM = 64
REC_CONV = 4
LRU_C = 8.0
FFN_CONV = 3
WINDOW_DILATIONS = ((128, 1), (512, 4), (2048, 16))
DIL_MID, DIL_FAR = 4, 16
BLOCK = 128
ROPE_THETA = 10000.0
EPS = 1e-6
NEG_INF = -1e30

LANES = 128
SUBLANES = 8
VMEM_LIMIT = 56 * 1024 * 1024

F32 = jnp.float32
BF16 = jnp.bfloat16


def _cparams(sem, flags=None):
    return pltpu.CompilerParams(dimension_semantics=sem, vmem_limit_bytes=VMEM_LIMIT, flags=flags)


_GELU_C = float(np.float32(np.sqrt(2.0 / np.pi)))
_GELU_CK = _GELU_C * float(np.float32(0.044715))


def _gelu_times(g, other):
    t = jnp.tanh(g * (_GELU_C + _GELU_CK * (g * g)))
    half = (0.5 * g) * other
    return half + half * t


def _const_spec(shape):
    nd = len(shape)
    return pl.BlockSpec(shape, lambda *_: (0,) * nd)


def _inproj_kernel(x_ref, pos_ref, gmix_ref, w_ref, qg_ref, kg_ref, invf_ref,
                   q1, k1, v1, q4, k4, v4, q16, k16, v16, xr_ref, gr_ref, proj, s1, s4, *, aw, rw, nsub):
    tm = x_ref.shape[0]
    rows_per = tm // nsub
    nslab = aw // LANES
    lane = lax.broadcasted_iota(jnp.int32, (rows_per, LANES), 1)
    head_a = lane < HEAD_DIM
    first_half = (lane & (HEAD_DIM // 2)) == 0
    n4 = rows_per // DIL_MID
    n16 = rows_per // DIL_FAR

    def matmuls(sub):
        rows = slice(sub * rows_per, (sub + 1) * rows_per)
        x = x_ref[rows, :]
        ms = jnp.mean(x * x, axis=-1, keepdims=True)
        h = (x * lax.rsqrt(ms + EPS) * gmix_ref[...]).astype(BF16)
        proj[sub % 2] = jnp.dot(h, w_ref[:, 0:3 * aw], preferred_element_type=F32)
        xr_ref[rows, :] = jnp.dot(h, w_ref[:, 3 * aw:3 * aw + rw], preferred_element_type=F32)
        gr_ref[rows, :] = jnp.dot(h, w_ref[:, 3 * aw + rw:3 * aw + 2 * rw], preferred_element_type=F32)

    def epilogue(sub):
        rows = slice(sub * rows_per, (sub + 1) * rows_per)
        slot = sub % 2
        pos_cols = [jnp.broadcast_to(pos_ref[r:r + 1, :], (LANES, LANES)).T
                    for r in range(sub * rows_per // LANES, (sub + 1) * rows_per // LANES)]
        ang = jnp.concatenate(pos_cols, axis=0) * invf_ref[...]
        cos = jnp.cos(ang)
        sin = jnp.sin(ang)
        sin_signed = jnp.where(first_half, -sin, sin)

        def head_norm_rotary(c0, g_ref, scale):
            outs = []
            for g in range(nslab):
                cols = slice(g * LANES, (g + 1) * LANES)
                pg = proj[slot, :, c0 + g * LANES:c0 + (g + 1) * LANES]
                sq = pg * pg
                s_a = jnp.sum(jnp.where(head_a, sq, 0.0), axis=-1, keepdims=True)
                s_b = jnp.sum(jnp.where(head_a, 0.0, sq), axis=-1, keepdims=True)
                ss_h = jnp.where(head_a, s_a, s_b)
                xg = pg * lax.rsqrt(ss_h + HEAD_DIM * EPS) * (g_ref[:, cols] * (scale * HEAD_DIM ** 0.5))
                swapped = jnp.where(first_half, pltpu.roll(xg, LANES - HEAD_DIM // 2, 1),
                                    pltpu.roll(xg, HEAD_DIM // 2, 1))
                outs.append(xg * cos + swapped * sin_signed)
            return outs

        def emit(which, slabs, o1, o4, o16):
            for g, val in enumerate(slabs):
                cols = slice(g * LANES, (g + 1) * LANES)
                o1[rows, cols] = val.astype(BF16)
                s1[slot, which, g] = val
                for c in range(DIL_MID):
                    t4 = s1[slot, which, g, pl.ds(c, n4, stride=DIL_MID), :]
                    o4[c, sub * n4:(sub + 1) * n4, cols] = t4.astype(BF16)
                    s4[slot, which, g, c * n4:(c + 1) * n4, :] = t4
                for c4 in range(DIL_MID):
                    for cp in range(DIL_FAR // DIL_MID):
                        t16 = s4[slot, which, g, pl.ds(c4 * n4 + cp, n16, stride=DIL_MID), :]
                        o16[c4 + DIL_MID * cp, sub * n16:(sub + 1) * n16, cols] = t16.astype(BF16)

        emit(0, head_norm_rotary(0, qg_ref, HEAD_DIM ** -0.5), q1, q4, q16)
        emit(1, head_norm_rotary(aw, kg_ref, 1.0), k1, k4, k16)
        emit(2, [proj[slot, :, 2 * aw + g * LANES:2 * aw + (g + 1) * LANES] for g in range(nslab)], v1, v4, v16)

    matmuls(0)
    for sub in range(nsub):
        if sub + 1 < nsub:
            matmuls(sub + 1)
        epilogue(sub)


def _inproj(x, pos, g_mix, w_in_b, qg, kg, invf, *, aw, rw, tm=1024, nsub=4):
    b, s, d = x.shape
    rows_per = tm // nsub
    assert rows_per % (2 * SUBLANES * DIL_FAR) == 0
    stage = pltpu.VMEM((2, 3, aw // LANES, rows_per, LANES), F32)
    row = lambda w: pl.BlockSpec((None, tm, w), lambda bi, i: (bi, i, 0))
    grp = lambda dil: pl.BlockSpec((None, dil, tm // dil, aw), lambda bi, i: (bi, 0, i, 0))
    grp_shape = lambda dil: jax.ShapeDtypeStruct((b, dil, s // dil, aw), BF16)
    tok_shape = jax.ShapeDtypeStruct((b, s, aw), BF16)
    return pl.pallas_call(
        functools.partial(_inproj_kernel, aw=aw, rw=rw, nsub=nsub),
        grid=(b, s // tm),
        in_specs=[row(d), pl.BlockSpec((None, tm // LANES, LANES), lambda bi, i: (bi, i, 0)),
                  _const_spec((1, d)), _const_spec(w_in_b.shape),
                  _const_spec((1, aw)), _const_spec((1, aw)),
                  _const_spec((1, LANES))],
        out_specs=[row(aw)] * 3 + [grp(DIL_MID)] * 3 + [grp(DIL_FAR)] * 3 + [row(rw)] * 2,
        out_shape=[tok_shape] * 3 + [grp_shape(DIL_MID)] * 3 + [grp_shape(DIL_FAR)] * 3
                  + [jax.ShapeDtypeStruct((b, s, rw), F32)] * 2,
        scratch_shapes=[pltpu.VMEM((2, rows_per, 3 * aw), F32), stage, stage],
        compiler_params=_cparams(("parallel", "parallel")),
        name="inproj",
    )(x, pos, g_mix, w_in_b, qg, kg, invf)


GROUP_BLOCKS = 8


def _attn_kernel(q1, k1, v1, q4, k4, v4, q16, k16, v16, out_ref,
                 va1, vb1, va4, vb4, va16, vb16, acc_s, m_s, l_s, tmp_a, tmp_m, tmp_l,
                 bias_s, s_a, s_b, m_a, m_b, *, span):
    seq = out_ref.shape[0]
    grp = seq // DIL_MID
    gb = GROUP_BLOCKS
    lane = lax.broadcasted_iota(jnp.int32, (BLOCK, LANES), 1)
    low_head = lane < HEAD_DIM
    high_head = jnp.logical_not(low_head)

    qi = lax.broadcasted_iota(jnp.int32, (BLOCK, 2 * BLOCK), 0)
    kj = lax.broadcasted_iota(jnp.int32, (BLOCK, 2 * BLOCK), 1)
    for e in range(2):
        rel = qi - kj + e * BLOCK
        bias_s[e] = jnp.where((rel >= 0) & (rel <= span), 0.0, NEG_INF).astype(F32)

    for v_ref, va, vb in ((v1, va1, vb1), (v4, va4, vb4), (v16, va16, vb16)):
        for c in range(v_ref.shape[0]):
            vv = v_ref[c]
            lo = lax.broadcasted_iota(jnp.int32, vv.shape, 1) < HEAD_DIM
            one = jnp.ones_like(vv)
            va[c] = jnp.where(lo, vv, one)
            vb[c] = jnp.where(lo, one, vv)

    def rows_of(i):
        return i * BLOCK, max(i - 1, 0) * BLOCK, min(i, 1)

    def scores(group, bufs):
        q_ref, k_ref, _, _, coords, _ = group
        s_buf, m_buf = bufs
        for t, (c, i) in enumerate(coords):
            r0, start, e = rows_of(i)
            qp = q_ref[c, r0:r0 + BLOCK, :]
            ks = k_ref[c, start:start + 2 * BLOCK, :]
            for h, sel in enumerate((low_head, high_head)):
                qm = jnp.where(sel, qp, jnp.zeros_like(qp))
                s = lax.dot_general(qm, ks, (((1,), (1,)), ((), ())), preferred_element_type=F32)
                s = s + bias_s[e]
                s_buf[2 * t + h] = s
                m = jnp.max(jnp.maximum(s[:, 0:BLOCK], s[:, BLOCK:]), axis=-1, keepdims=True)
                m_buf[2 * t + h] = jnp.broadcast_to(m, (BLOCK, LANES))

    def finish(group, bufs):
        _, _, va, vb, coords, consume = group
        s_buf, m_buf = bufs
        for t, (c, i) in enumerate(coords):
            _, start, _ = rows_of(i)
            pvs = []
            for h, v_ref in enumerate((va, vb)):
                hb = 2 * t + h
                m = m_buf[hb]
                p = jnp.concatenate([jnp.exp(s_buf[hb, :, 0:BLOCK] - m).astype(BF16),
                                     jnp.exp(s_buf[hb, :, BLOCK:] - m).astype(BF16)], axis=1)
                pvs.append(jnp.dot(p, v_ref[c, start:start + 2 * BLOCK, :], preferred_element_type=F32))
            acc = jnp.where(low_head, pvs[0], pvs[1])
            l = pltpu.roll(jnp.where(low_head, pvs[1], pvs[0]), HEAD_DIM, 1)
            m = jnp.where(low_head, m_buf[2 * t], m_buf[2 * t + 1])
            consume(c, i, (acc, m, l))

    def merge(old, new):
        acc0, m0, l0 = old
        acc1, m1, l1 = new
        mn = jnp.maximum(m0, m1)
        e0 = jnp.exp(m0 - mn)
        e1 = jnp.exp(m1 - mn)
        return e0 * acc0 + e1 * acc1, mn, e0 * l0 + e1 * l1

    def mid_store(c, i, new):
        r0 = c * grp + i * BLOCK
        for ref, val in zip((acc_s, m_s, l_s), new):
            ref[r0:r0 + BLOCK, :] = val

    ratio = DIL_FAR // DIL_MID
    far_blocks = seq // DIL_FAR // BLOCK

    def far_merge(c, i, new):
        row0 = (c % DIL_MID) * grp + c // DIL_MID + i * (BLOCK * ratio)
        rows = pl.ds(row0, BLOCK, stride=ratio)
        merged = merge((acc_s[rows, :], m_s[rows, :], l_s[rows, :]), new)
        for ref, val in zip((acc_s, m_s, l_s), merged):
            ref[rows, :] = val

    def near_emit(c, i, new):
        per = BLOCK // DIL_MID
        for r in range(DIL_MID):
            src = slice(r * grp + i * per, r * grp + (i + 1) * per)
            dst = pl.ds(r, per, stride=DIL_MID)
            tmp_a[dst, :] = acc_s[src, :]
            tmp_m[dst, :] = m_s[src, :]
            tmp_l[dst, :] = l_s[src, :]
        acc, _, l = merge((tmp_a[...], tmp_m[...], tmp_l[...]), new)
        out_ref[i * BLOCK:(i + 1) * BLOCK, :] = (acc / l).astype(out_ref.dtype)

    groups = [(q4, k4, va4, vb4, [(c, t) for t in range(gb)], mid_store) for c in range(DIL_MID)]
    groups += [(q16, k16, va16, vb16,
                [(gi * (gb // far_blocks) + t // far_blocks, t % far_blocks) for t in range(gb)], far_merge)
               for gi in range(DIL_FAR * far_blocks // gb)]
    groups += [(q1, k1, va1, vb1, [(0, gi * gb + t) for t in range(gb)], near_emit)
               for gi in range(seq // BLOCK // gb)]

    s_bufs = ((s_a, m_a), (s_b, m_b))
    scores(groups[0], s_bufs[0])
    for k, group in enumerate(groups):
        if k + 1 < len(groups):
            scores(groups[k + 1], s_bufs[(k + 1) % 2])
        finish(group, s_bufs[k % 2])


def _attention(q1, k1, v1, q4, k4, v4, q16, k16, v16):
    b, s, w = q1.shape
    assert WINDOW_DILATIONS == ((BLOCK, 1), (BLOCK * DIL_MID, DIL_MID), (BLOCK * DIL_FAR, DIL_FAR))
    assert s % (BLOCK * DIL_FAR) == 0 and s // DIL_FAR >= 2 * BLOCK
    gb = GROUP_BLOCKS
    assert s // DIL_MID // BLOCK == gb and gb % (s // DIL_FAR // BLOCK) == 0 and (s // BLOCK) % gb == 0
    tok4 = lambda t: t.reshape(b, 1, s, w)
    spec = lambda dil: pl.BlockSpec((None, dil, s // dil, LANES), lambda bi, g: (bi, 0, 0, g))
    vshape = lambda dil: pltpu.VMEM((dil, s // dil, LANES), BF16)
    return pl.pallas_call(
        functools.partial(_attn_kernel, span=BLOCK),
        grid=(b, w // LANES),
        in_specs=[spec(1)] * 3 + [spec(DIL_MID)] * 3 + [spec(DIL_FAR)] * 3,
        out_specs=pl.BlockSpec((None, s, LANES), lambda bi, g: (bi, 0, g)),
        out_shape=jax.ShapeDtypeStruct((b, s, w), BF16),
        scratch_shapes=[vshape(1), vshape(1), vshape(DIL_MID), vshape(DIL_MID), vshape(DIL_FAR),
                        vshape(DIL_FAR)]
                       + [pltpu.VMEM((s, LANES), F32)] * 3 + [pltpu.VMEM((BLOCK, LANES), F32)] * 3
                       + [pltpu.VMEM((2, BLOCK, 2 * BLOCK), F32),
                          pltpu.VMEM((2 * gb, BLOCK, 2 * BLOCK), F32),
                          pltpu.VMEM((2 * gb, BLOCK, 2 * BLOCK), F32),
                          pltpu.VMEM((2 * gb, BLOCK, LANES), F32),
                          pltpu.VMEM((2 * gb, BLOCK, LANES), F32)],
        compiler_params=_cparams(("parallel", "parallel")),
        name="attn",
    )(tok4(q1), tok4(k1), tok4(v1), q4, k4, v4, q16, k16, v16)


def _rec_kernel(xr_ref, gr_ref, cw_ref, cb_ref, wr_ref, br_ref, wi_ref, bi_ref, lam_ref, g_ref,
                out_ref, xe, a_s, h_s, carry, *, pitch):
    ts = xr_ref.shape[0]
    pad = SUBLANES

    @pl.when(pl.program_id(1) == 0)
    def _():
        xe[0:pad, :] = jnp.zeros((pad, xe.shape[1]), F32)
        carry[...] = jnp.zeros_like(carry)

    x = xr_ref[...]
    xe[pad:, :] = x
    xc = cb_ref[...] + cw_ref[REC_CONV - 1:REC_CONV, :] * x
    for kk in range(1, REC_CONV):
        xc = xc + cw_ref[REC_CONV - 1 - kk:REC_CONV - kk, :] * xe[pl.ds(pad - kk, ts), :]
    xe[0:pad, :] = x[ts - pad:, :]

    xb = xc.astype(BF16)
    t_r = jnp.tanh(jnp.dot(xb, wr_ref[...], preferred_element_type=F32) + 0.5 * br_ref[...])
    t_i = jnp.tanh(jnp.dot(xb, wi_ref[...], preferred_element_type=F32) + 0.5 * bi_ref[...])
    z = -lam_ref[...]
    softplus = jnp.maximum(z, 0.0) + jnp.log1p(jnp.exp(-jnp.abs(z)))
    half_c = (-0.5 * LRU_C) * softplus
    log_a = half_c + half_c * t_r
    a = jnp.exp(log_a)
    w = jnp.tanh(-log_a) * (a * a + 1.0)
    half_x = 0.5 * xc
    u = jnp.where(w > 0.0, w * lax.rsqrt(w), 0.0) * (half_x + half_x * t_i)

    nch = SUBLANES
    clen = ts // nch
    nslab = a_s.shape[0]
    for g in range(nslab):
        cols = slice(g * LANES, (g + 1) * LANES)
        for j in range(nch):
            a_s[g, j * pitch:j * pitch + clen, :] = a[j * clen:(j + 1) * clen, cols]
            h_s[g, j * pitch:j * pitch + clen, :] = u[j * clen:(j + 1) * clen, cols]

    def step(s, hp):
        hs, ps = hp
        rows = pl.ds(s, nch, stride=pitch)
        new_h, new_p = [], []
        for g in range(nslab):
            av = a_s[g, rows, :]
            hv = av * hs[g] + h_s[g, rows, :]
            pv = av * ps[g]
            h_s[g, rows, :] = hv
            a_s[g, rows, :] = pv
            new_h.append(hv)
            new_p.append(pv)
        return tuple(new_h), tuple(new_p)

    zero = jnp.zeros((nch, LANES), F32)
    h_end, p_end = lax.fori_loop(0, clen, step, ((zero,) * nslab, (zero + 1.0,) * nslab))

    carry_in = []
    for g in range(nslab):
        cols = slice(g * LANES, (g + 1) * LANES)
        cj = carry[:, cols]
        per_chunk = []
        for j in range(nch):
            per_chunk.append(cj)
            cj = h_end[g][j:j + 1, :] + p_end[g][j:j + 1, :] * cj
        carry[:, cols] = cj
        carry_in.append(per_chunk)

    for j in range(nch):
        rows = slice(j * pitch, j * pitch + clen)
        h = jnp.concatenate([h_s[g, rows, :] + a_s[g, rows, :] * carry_in[g][j] for g in range(nslab)], axis=1)
        y = _gelu_times(gr_ref[j * clen:(j + 1) * clen, :], h)
        ms = jnp.mean(y * y, axis=-1, keepdims=True)
        out_ref[j * clen:(j + 1) * clen, :] = (y * lax.rsqrt(ms + EPS) * g_ref[...]).astype(out_ref.dtype)


def _rec(xr, gr, cw, cb, wr_bd, br, wi_bd, bi, lam, g, *, ts=1024):
    b, s, w = xr.shape
    blk = pl.BlockSpec((None, ts, w), lambda bi_, t: (bi_, t, 0))
    vec = _const_spec((1, w))
    pitch = ts // SUBLANES + 4
    assert (ts // SUBLANES) % 8 == 0
    scan_buf = pltpu.VMEM((w // LANES, SUBLANES * pitch, LANES), F32)
    return pl.pallas_call(
        functools.partial(_rec_kernel, pitch=pitch),
        grid=(b, s // ts),
        in_specs=[blk, blk, _const_spec(cw.shape), vec, _const_spec(wr_bd.shape), vec,
                  _const_spec(wi_bd.shape), vec, vec, vec],
        out_specs=blk,
        out_shape=jax.ShapeDtypeStruct((b, s, w), BF16),
        scratch_shapes=[pltpu.VMEM((ts + SUBLANES, w), F32), scan_buf, scan_buf,
                        pltpu.VMEM((1, w), F32)],
        compiler_params=_cparams(("parallel", "arbitrary")),
        name="rec",
    )(xr, gr, cw, cb, wr_bd, br, wi_bd, bi, lam, g)


def _ffn_kernel(x_ref, attn_ref, rec_ref, ga_ref, wo_ref, g_ref, wup_ref, cw_ref, cb_ref, wdn_ref,
                out_ref, ua, ub, tail, acc, h_s, act_a, act_b, *, cf):
    tm = x_ref.shape[0]
    pad = SUBLANES
    d_ff = wdn_ref.shape[0]
    bounds = list(range(0, d_ff + 1, cf))
    nchunks = len(bounds) - 1
    ubufs = (ua, ub)
    acts = (act_a, act_b)

    @pl.when(pl.program_id(1) == 0)
    def _():
        tail[...] = jnp.zeros_like(tail)
        for buf in ubufs:
            buf[:, pad + tm:, :] = jnp.zeros((FFN_CONV, pad, buf.shape[2]), F32)

    attn = attn_ref[...].astype(F32)
    ms = jnp.mean(attn * attn, axis=-1, keepdims=True)
    aw = attn.shape[1]
    h_s[:, 0:aw] = (attn * lax.rsqrt(ms + EPS) * ga_ref[...]).astype(BF16)
    h_s[:, aw:] = rec_ref[...]
    x = x_ref[...] + jnp.dot(h_s[...], wo_ref[...], preferred_element_type=F32)
    ms = jnp.mean(x * x, axis=-1, keepdims=True)
    h_s[...] = (x * lax.rsqrt(ms + EPS) * g_ref[...]).astype(BF16)
    acc[...] = x

    def halves(j):
        lo, hi = bounds[j], bounds[j + 1]
        for half in range(2):
            yield slice(half * d_ff + lo, half * d_ff + hi), slice(half * cf, half * cf + hi - lo)

    def up(j):
        buf = ubufs[j % 2]
        for cols, dst in halves(j):
            u = jnp.dot(h_s[...], wup_ref[:, cols], preferred_element_type=F32)
            buf[0, pad:pad + tm, dst] = u
            for kk in range(1, FFN_CONV):
                buf[kk, pad:2 * pad, dst] = tail[kk - 1, :, cols]
                buf[kk, pad + kk:pad + kk + tm, dst] = u
                tail[kk - 1, :, cols] = buf[kk, pad + tm:2 * pad + tm, dst]

    def elementwise(j):
        buf = ubufs[j % 2]
        parts = []
        for cols, dst in halves(j):
            uc = cb_ref[:, cols]
            for kk in range(FFN_CONV):
                uc = uc + cw_ref[FFN_CONV - 1 - kk:FFN_CONV - kk, cols] * buf[kk, pad:pad + tm, dst]
            parts.append(uc)
        acts[j % 2][:, 0:bounds[j + 1] - bounds[j]] = _gelu_times(parts[0], parts[1]).astype(BF16)

    def down(j):
        lo, hi = bounds[j], bounds[j + 1]
        acc[...] += jnp.dot(acts[j % 2][:, 0:hi - lo], wdn_ref[lo:hi, :], preferred_element_type=F32)

    up(0)
    for j in range(nchunks + 1):
        if j + 1 < nchunks:
            up(j + 1)
        if j < nchunks:
            elementwise(j)
        if j >= 1:
            down(j - 1)
    out_ref[...] = acc[...]


def _ffn(x, attn, rec, ga, wo, g, wup_b, cw, cb, wdn_b, *, tm=512, cf=512):
    b, s, d = x.shape
    d_ff = wdn_b.shape[0]
    assert d_ff % cf == 0
    row = lambda w: pl.BlockSpec((None, tm, w), lambda bi, t: (bi, t, 0))
    blk = row(d)
    return pl.pallas_call(
        functools.partial(_ffn_kernel, cf=cf),
        grid=(b, s // tm),
        in_specs=[blk, row(attn.shape[-1]), row(rec.shape[-1]), _const_spec(ga.shape),
                  _const_spec(wo.shape),
                  _const_spec((1, d)), _const_spec(wup_b.shape), _const_spec(cw.shape),
                  _const_spec(cb.shape), _const_spec(wdn_b.shape)],
        out_specs=blk,
        out_shape=jax.ShapeDtypeStruct((b, s, d), F32),
        scratch_shapes=[pltpu.VMEM((FFN_CONV, tm + 2 * SUBLANES, 2 * cf), F32)] * 2
                       + [pltpu.VMEM((FFN_CONV - 1, SUBLANES, 2 * d_ff), F32), pltpu.VMEM((tm, d), F32),
                          pltpu.VMEM((tm, d), BF16), pltpu.VMEM((tm, cf), BF16),
                          pltpu.VMEM((tm, cf), BF16)],
        compiler_params=_cparams(("parallel", "arbitrary")),
        name="ffn",
    )(x, attn, rec, ga, wo, g, wup_b, cw, cb, wdn_b)


def _block_diag(w):
    n, c, _ = w.shape
    eye = jnp.eye(n, dtype=w.dtype)
    return (eye[:, None, :, None] * w[:, :, None, :]).reshape(n * c, n * c)


def kernel(x, positions, g_mix, w_in, q_norm_g, k_norm_g, rec_conv_w, rec_conv_b, w_rg, b_rg, w_ig,
           b_ig, lru_lambda, g_attn_out, g_rec_out, w_out, g_ffn, w_up, ffn_conv_w, ffn_conv_b, w_down):
    bsz, s, d = x.shape
    t = bsz * s
    depth = w_in.shape[0]
    rw = rec_conv_w.shape[-1]
    aw = w_out.shape[1] - rw
    n_heads = aw // HEAD_DIM

    half = HEAD_DIM // 2
    inv_freq = ROPE_THETA ** (-jnp.arange(half, dtype=F32) / half)
    invf = jnp.tile(inv_freq, LANES // half).reshape(1, LANES)
    pos = positions.astype(F32).reshape(bsz, s // LANES, LANES)

    for layer in range(depth):
        *qkv, xr, gr = _inproj(
            x, pos, g_mix[layer].reshape(1, d), w_in[layer].astype(BF16),
            jnp.tile(q_norm_g[layer], n_heads).reshape(1, aw),
            jnp.tile(k_norm_g[layer], n_heads).reshape(1, aw), invf, aw=aw, rw=rw)

        attn = _attention(*qkv)

        rec = _rec(xr, gr, rec_conv_w[layer],
                   rec_conv_b[layer].reshape(1, rw), _block_diag(0.5 * w_rg[layer]).astype(BF16),
                   b_rg[layer].reshape(1, rw), _block_diag(0.5 * w_ig[layer]).astype(BF16),
                   b_ig[layer].reshape(1, rw), lru_lambda[layer].reshape(1, rw),
                   g_rec_out[layer].reshape(1, rw))

        w_out_b = w_out[layer].astype(BF16)
        x = _ffn(x, attn, rec, g_attn_out[layer].reshape(1, aw), w_out_b,
                 g_ffn[layer].reshape(1, d), w_up[layer].astype(BF16),
                 ffn_conv_w[layer], ffn_conv_b[layer].reshape(1, -1), w_down[layer].astype(BF16))
    return x
```
